```python
import jax
import jax.numpy as jnp
from jax import lax
import numpy as np

D_MODEL = 1024
BATCH = 32
SEQ = 2048
DEPTH = 4
DEC_BATCH = 32
DEC_SEQ = 16
PAST_LEN = 4096

CHUNK = 64
N_MIXERS = 3
N_A = (DEPTH + 2) // 3
N_B = (DEPTH + 1) // 3
N_C = DEPTH // 3
A_HEADS = 16
A_HEAD_DIM = D_MODEL // A_HEADS
A_LEFT_CHUNKS = 8
A_BAND = (A_LEFT_CHUNKS + 1) * CHUNK
A_WINDOW = A_LEFT_CHUNKS * CHUNK
A_REL_CLIP = 128
B_HEADS = 16
B_HEAD_DIM = D_MODEL // B_HEADS
Q_BLOCK = 128
C_HEADS = 4
C_KEY_DIM = D_MODEL // C_HEADS
C_VAL_DIM = 2 * C_KEY_DIM
C_QK_WIDTH = C_HEADS * C_KEY_DIM
C_VAL_WIDTH = C_HEADS * C_VAL_DIM
ROPE_BASE = 10000.0
D_FF = 2816
NORM_EPS = 1e-6
GN_EPS = 1e-5
NEG_INF = -1e30

kernel_name = "hybrid_streaming_encoder_step"


def rmsnorm(x, g):
    xf = x.astype(jnp.float32)
    y = xf * lax.rsqrt(jnp.mean(xf * xf, axis=-1, keepdims=True) + NORM_EPS)
    return (y * g.astype(jnp.float32)).astype(x.dtype)


def swiglu(h, w_in, w_out):
    gate, up = jnp.split(h @ w_in, 2, axis=-1)
    return (jax.nn.silu(gate) * up) @ w_out


def rel_bias(table, n_q, n_k, offset):
    rel = offset + jnp.arange(n_q)[:, None] - jnp.arange(n_k)[None, :]
    idx = jnp.clip(rel, -A_REL_CLIP, A_REL_CLIP) + A_REL_CLIP
    return table[:, idx].astype(jnp.float32)


def softmax_attend(q, k, v, bias, mask):
    scale = q.shape[-1] ** -0.5
    s = jnp.einsum("bqhd,bkhd->bhqk", q, k).astype(jnp.float32) * scale + bias
    s = jnp.where(mask, s, NEG_INF)
    p = jax.nn.softmax(s, axis=-1).astype(v.dtype)
    return jnp.einsum("bhqk,bkhd->bqhd", p, v)


def chunk_qkv(h, w_in, g_q, g_k):
    b, l, _ = h.shape
    q, k, v = jnp.split(h @ w_in, 3, axis=-1)
    q = rmsnorm(q.reshape(b, l, A_HEADS, A_HEAD_DIM), g_q)
    k = rmsnorm(k.reshape(b, l, A_HEADS, A_HEAD_DIM), g_k)
    return q, k, v.reshape(b, l, A_HEADS, A_HEAD_DIM)


def chunk_attn_prompt(h, w_in, g_q, g_k, rel_table, w_out):
    b, s, _ = h.shape
    q, k, v = chunk_qkv(h, w_in, g_q, g_k)
    n_chunks = s // CHUNK
    pad = ((0, 0), (A_WINDOW, 0), (0, 0), (0, 0))
    k_pad, v_pad = jnp.pad(k, pad), jnp.pad(v, pad)
    q_chunks = jnp.moveaxis(q.reshape(b, n_chunks, CHUNK, A_HEADS, A_HEAD_DIM), 1, 0)
    bias = rel_bias(rel_table, CHUNK, A_BAND, A_BAND - CHUNK)
    key_off = jnp.arange(A_BAND)

    def one_chunk(args):
        c, q_c = args
        k_band = lax.dynamic_slice_in_dim(k_pad, c * CHUNK, A_BAND, axis=1)
        v_band = lax.dynamic_slice_in_dim(v_pad, c * CHUNK, A_BAND, axis=1)
        valid = (c - A_LEFT_CHUNKS) * CHUNK + key_off >= 0
        return softmax_attend(q_c, k_band, v_band, bias, valid)

    o = lax.map(one_chunk, (jnp.arange(n_chunks), q_chunks))
    o = jnp.moveaxis(o, 0, 1).reshape(b, s, D_MODEL)
    keep = min(A_WINDOW, s)
    return o @ w_out, k[:, s - keep:], v[:, s - keep:]


def chunk_attn_sample(h, cache_k, cache_v, w_in, g_q, g_k, rel_table, w_out):
    b, t, _ = h.shape
    q, k, v = chunk_qkv(h, w_in, g_q, g_k)
    w = cache_k.shape[1]
    k_all = jnp.concatenate([cache_k.astype(k.dtype), k], axis=1)
    v_all = jnp.concatenate([cache_v.astype(v.dtype), v], axis=1)
    bias = rel_bias(rel_table, t, w + t, w)
    o = softmax_attend(q, k_all, v_all, bias, True).reshape(b, t, D_MODEL)
    return o @ w_out, k, v


def fox_project(h, w_in, bias_f, g_q, g_k):
    b, l, _ = h.shape
    q, k, v, f = jnp.split(h @ w_in, [D_MODEL, 2 * D_MODEL, 3 * D_MODEL], axis=-1)
    q = rmsnorm(q.reshape(b, l, B_HEADS, B_HEAD_DIM), g_q)
    k = rmsnorm(k.reshape(b, l, B_HEADS, B_HEAD_DIM), g_k)
    v = v.reshape(b, l, B_HEADS, B_HEAD_DIM)
    log_f = jax.nn.log_sigmoid(f.astype(jnp.float32) + bias_f.astype(jnp.float32))
    return q, k, v, log_f


def fox_prompt(h, w_in, bias_f, g_q, g_k, w_out):
    b, s, _ = h.shape
    q, k, v, log_f = fox_project(h, w_in, bias_f, g_q, g_k)
    cum = jnp.moveaxis(jnp.cumsum(log_f, axis=1), 2, 1)
    outs = []
    for start in range(0, s, Q_BLOCK):
        end = start + Q_BLOCK
        decay = cum[:, :, start:end, None] - cum[:, :, None, :end]
        causal = jnp.arange(end)[None, :] <= (start + jnp.arange(Q_BLOCK))[:, None]
        outs.append(softmax_attend(q[:, start:end], k[:, :end], v[:, :end], decay, causal))
    o = jnp.concatenate(outs, axis=1).reshape(b, s, D_MODEL)
    return o @ w_out, k, v, log_f


def fox_sample(h, cache_k, cache_v, cache_logf, w_in, bias_f, g_q, g_k, w_out):
    b, t, _ = h.shape
    q, k, v, log_f = fox_project(h, w_in, bias_f, g_q, g_k)
    p_len = cache_k.shape[1]
    k_all = jnp.concatenate([cache_k.astype(k.dtype), k], axis=1)
    v_all = jnp.concatenate([cache_v.astype(v.dtype), v], axis=1)
    logf_all = jnp.concatenate([cache_logf.astype(jnp.float32), log_f], axis=1)
    cum = jnp.moveaxis(jnp.cumsum(logf_all, axis=1), 2, 1)
    decay = cum[:, :, p_len:, None] - cum[:, :, None, :]
    causal = jnp.arange(p_len + t)[None, :] <= (p_len + jnp.arange(t))[:, None]
    o = softmax_attend(q, k_all, v_all, decay, causal).reshape(b, t, D_MODEL)
    return o @ w_out, k, v, log_f


def rotary(x, pos):
    half = x.shape[-1] // 2
    inv_freq = ROPE_BASE ** (-jnp.arange(half, dtype=jnp.float32) / half)
    ang = pos.astype(jnp.float32)[:, None] * inv_freq[None, :]
    cos, sin = jnp.cos(ang)[None, :, None, :], jnp.sin(ang)[None, :, None, :]
    xf = x.astype(jnp.float32)
    x1, x2 = xf[..., :half], xf[..., half:]
    return jnp.concatenate([x1 * cos - x2 * sin, x2 * cos + x1 * sin], axis=-1)


def retention_log_gamma():
    return jnp.log1p(-jnp.exp2(-5.0 - jnp.arange(C_HEADS, dtype=jnp.float32)))


def ret_project(h, pos, w_in):
    b, l, _ = h.shape
    q, k, v, g = jnp.split(h @ w_in, [C_QK_WIDTH, 2 * C_QK_WIDTH, 2 * C_QK_WIDTH + C_VAL_WIDTH], axis=-1)
    q = rotary(q.reshape(b, l, C_HEADS, C_KEY_DIM), pos)
    k = rotary(k.reshape(b, l, C_HEADS, C_KEY_DIM), pos) * (C_KEY_DIM ** -0.5)
    v = v.reshape(b, l, C_HEADS, C_VAL_DIM).astype(jnp.float32)
    return q, k, v, g


def retention_block(q, k, v, state, log_gamma):
    l = q.shape[1]
    idx = jnp.arange(l, dtype=jnp.float32)
    diff = idx[:, None] - idx[None, :]
    decay = jnp.where(diff >= 0, jnp.exp(jnp.maximum(diff, 0.0)[None] * log_gamma[:, None, None]), 0.0)
    scores = jnp.einsum("blhd,bmhd->bhlm", q, k) * decay
    intra = jnp.einsum("bhlm,bmhv->blhv", scores, v)
    q_decay = jnp.exp((idx + 1.0)[:, None] * log_gamma[None, :])
    cross = jnp.einsum("blhd,bhdv->blhv", q, state) * q_decay[None, :, :, None]
    k_decay = jnp.exp((l - 1.0 - idx)[:, None] * log_gamma[None, :])
    new_state = (jnp.exp(l * log_gamma)[None, :, None, None] * state
                 + jnp.einsum("blhd,blhv->bhdv", k * k_decay[None, :, :, None], v))
    return intra + cross, new_state


def ret_output(o, g, gn_g, w_out):
    b, l = o.shape[:2]
    mu = jnp.mean(o, axis=-1, keepdims=True)
    var = jnp.mean(jnp.square(o - mu), axis=-1, keepdims=True)
    o = ((o - mu) * lax.rsqrt(var + GN_EPS)).reshape(b, l, C_VAL_WIDTH) * gn_g.astype(jnp.float32)
    y = jax.nn.silu(g.astype(jnp.float32)) * o
    return y.astype(g.dtype) @ w_out


def retention_prompt(h, pos, w_in, gn_g, w_out):
    b, s, _ = h.shape
    q, k, v, g = ret_project(h, pos, w_in)
    n_chunks = s // CHUNK

    def to_chunks(x):
        return jnp.moveaxis(x.reshape((b, n_chunks, CHUNK) + x.shape[2:]), 1, 0)

    log_gamma = retention_log_gamma()

    def step(state, blk):
        o, state = retention_block(blk[0], blk[1], blk[2], state, log_gamma)
        return state, o

    state0 = jnp.zeros((b, C_HEADS, C_KEY_DIM, C_VAL_DIM), jnp.float32)
    state, o = lax.scan(step, state0, (to_chunks(q), to_chunks(k), to_chunks(v)))
    o = jnp.moveaxis(o, 0, 1).reshape(b, s, C_HEADS, C_VAL_DIM)
    return ret_output(o, g, gn_g, w_out), state


def retention_sample(h, pos, state, w_in, gn_g, w_out):
    q, k, v, g = ret_project(h, pos, w_in)
    o, new_state = retention_block(q, k, v, state.astype(jnp.float32), retention_log_gamma())
    return ret_output(o, g, gn_g, w_out), new_state


def setup_inputs(seed: int = 0) -> dict:
    key = jax.random.key(seed)
    ks = jax.random.split(key, 24)
    f32 = jnp.float32

    def normal(k, shape, scale=1.0):
        return scale * jax.random.normal(k, shape, f32)

    def dense(k, shape):
        return normal(k, shape, shape[-2] ** -0.5)

    def gain(k, shape):
        return 1.0 + normal(k, shape, 0.05)

    win = min(A_WINDOW, PAST_LEN)
    return {
        "x_prompt": normal(ks[0], (BATCH, SEQ, D_MODEL)),
        "x_sample": normal(ks[1], (DEC_BATCH, DEC_SEQ, D_MODEL)),
        "cache_chunk_k": normal(ks[2], (N_A, DEC_BATCH, win, A_HEADS, A_HEAD_DIM)),
        "cache_chunk_v": normal(ks[3], (N_A, DEC_BATCH, win, A_HEADS, A_HEAD_DIM)),
        "cache_fox_k": normal(ks[4], (N_B, DEC_BATCH, PAST_LEN, B_HEADS, B_HEAD_DIM)),
        "cache_fox_v": normal(ks[5], (N_B, DEC_BATCH, PAST_LEN, B_HEADS, B_HEAD_DIM)),
        "cache_fox_logf": jax.nn.log_sigmoid(3.0 + normal(ks[6], (N_B, DEC_BATCH, PAST_LEN, B_HEADS))),
        "state_ret": normal(ks[7], (N_C, DEC_BATCH, C_HEADS, C_KEY_DIM, C_VAL_DIM), 0.5),
        "norm_g": gain(ks[8], (DEPTH, 4, D_MODEL)),
        "w_ffn_in": dense(ks[9], (DEPTH, 2, D_MODEL, 2 * D_FF)),
        "w_ffn_out": dense(ks[10], (DEPTH, 2, D_FF, D_MODEL)),
        "a_w_in": dense(ks[11], (N_A, D_MODEL, 3 * D_MODEL)),
        "a_g_q": gain(ks[12], (N_A, A_HEAD_DIM)),
        "a_g_k": gain(ks[13], (N_A, A_HEAD_DIM)),
        "a_rel_table": normal(ks[14], (N_A, A_HEADS, 2 * A_REL_CLIP + 1), 0.5),
        "a_w_out": dense(ks[15], (N_A, D_MODEL, D_MODEL)),
        "b_w_in": dense(ks[16], (N_B, D_MODEL, 3 * D_MODEL + B_HEADS)),
        "b_bias_f": 1.0 + 5.0 * jax.random.uniform(ks[17], (N_B, B_HEADS), f32),
        "b_g_q": gain(ks[18], (N_B, B_HEAD_DIM)),
        "b_g_k": gain(ks[19], (N_B, B_HEAD_DIM)),
        "b_w_out": dense(ks[20], (N_B, D_MODEL, D_MODEL)),
        "c_w_in": dense(ks[21], (N_C, D_MODEL, 2 * C_QK_WIDTH + 2 * C_VAL_WIDTH)),
        "c_gn_g": gain(ks[22], (N_C, C_VAL_WIDTH)),
        "c_w_out": dense(ks[23], (N_C, C_VAL_WIDTH, D_MODEL)),
    }


def reference(x_prompt, x_sample, cache_chunk_k, cache_chunk_v, cache_fox_k, cache_fox_v,
              cache_fox_logf, state_ret, norm_g, w_ffn_in, w_ffn_out,
              a_w_in, a_g_q, a_g_k, a_rel_table, a_w_out,
              b_w_in, b_bias_f, b_g_q, b_g_k, b_w_out,
              c_w_in, c_gn_g, c_w_out):
    seq = x_prompt.shape[1]
    dec_seq = x_sample.shape[1]
    past_len = cache_fox_k.shape[2]
    pos_prompt = jnp.arange(seq)
    pos_sample = past_len + jnp.arange(dec_seq)
    xp, xs = x_prompt, x_sample
    a_kp, a_vp, a_ks, a_vs = [], [], [], []
    b_kp, b_vp, b_fp, b_ks, b_vs, b_fs = [], [], [], [], [], []
    c_sp, c_ss = [], []
    for i in range(DEPTH):
        g = norm_g[i]
        xp = xp + 0.5 * swiglu(rmsnorm(xp, g[0]), w_ffn_in[i, 0], w_ffn_out[i, 0])
        xs = xs + 0.5 * swiglu(rmsnorm(xs, g[0]), w_ffn_in[i, 0], w_ffn_out[i, 0])
        hp, hs = rmsnorm(xp, g[1]), rmsnorm(xs, g[1])
        kind, j = i % N_MIXERS, i // N_MIXERS
        if kind == 0:
            mp, kp, vp = chunk_attn_prompt(hp, a_w_in[j], a_g_q[j], a_g_k[j], a_rel_table[j], a_w_out[j])
            ms, ksm, vsm = chunk_attn_sample(hs, cache_chunk_k[j], cache_chunk_v[j], a_w_in[j],
                                             a_g_q[j], a_g_k[j], a_rel_table[j], a_w_out[j])
            a_kp.append(kp)
            a_vp.append(vp)
            a_ks.append(ksm)
            a_vs.append(vsm)
        elif kind == 1:
            mp, kp, vp, fp = fox_prompt(hp, b_w_in[j], b_bias_f[j], b_g_q[j], b_g_k[j], b_w_out[j])
            ms, ksm, vsm, fsm = fox_sample(hs, cache_fox_k[j], cache_fox_v[j], cache_fox_logf[j],
                                           b_w_in[j], b_bias_f[j], b_g_q[j], b_g_k[j], b_w_out[j])
            b_kp.append(kp)
            b_vp.append(vp)
            b_fp.append(fp)
            b_ks.append(ksm)
            b_vs.append(vsm)
            b_fs.append(fsm)
        else:
            mp, sp = retention_prompt(hp, pos_prompt, c_w_in[j], c_gn_g[j], c_w_out[j])
            ms, ssm = retention_sample(hs, pos_sample, state_ret[j], c_w_in[j], c_gn_g[j], c_w_out[j])
            c_sp.append(sp)
            c_ss.append(ssm)
        xp = xp + mp
        xs = xs + ms
        xp = xp + 0.5 * swiglu(rmsnorm(xp, g[2]), w_ffn_in[i, 1], w_ffn_out[i, 1])
        xs = xs + 0.5 * swiglu(rmsnorm(xs, g[2]), w_ffn_in[i, 1], w_ffn_out[i, 1])
        xp = rmsnorm(xp, g[3])
        xs = rmsnorm(xs, g[3])
    chunk_k_prompt = jnp.stack(a_kp)
    chunk_v_prompt = jnp.stack(a_vp)
    chunk_k_sample = jnp.stack(a_ks)
    chunk_v_sample = jnp.stack(a_vs)
    fox_k_prompt = jnp.stack(b_kp)
    fox_v_prompt = jnp.stack(b_vp)
    fox_logf_prompt = jnp.stack(b_fp)
    fox_k_sample = jnp.stack(b_ks)
    fox_v_sample = jnp.stack(b_vs)
    fox_logf_sample = jnp.stack(b_fs)
    ret_state_prompt = jnp.stack(c_sp)
    ret_state_sample = jnp.stack(c_ss)
    return (xp, xs, chunk_k_prompt, chunk_v_prompt, chunk_k_sample, chunk_v_sample,
            fox_k_prompt, fox_v_prompt, fox_logf_prompt, fox_k_sample, fox_v_sample, fox_logf_sample,
            ret_state_prompt, ret_state_sample)
```

```python
import functools

import numpy as np
import jax
import jax.numpy as jnp
from jax import lax
from jax.experimental import pallas as pl
from jax.experimental.pallas import tpu as pltpu

CHUNK = 64
A_HEADS = 16
A_LEFT_CHUNKS = 8
A_WINDOW = A_LEFT_CHUNKS * CHUNK
A_BAND = A_WINDOW + CHUNK
A_REL_CLIP = 128
B_HEADS = 16
C_HEADS = 4
ROPE_BASE = 10000.0
NORM_EPS = 1e-6
GN_EPS = 1e-5
NEG_INF = -1e30
N_MIXERS = 3

LANES = 128
HEAD_PAIR = LANES
MXU_DIM = 256
VMEM_LIMIT = 56 * 1024 * 1024

BF16 = jnp.bfloat16
F32 = jnp.float32


def _params(n_axes, vmem=VMEM_LIMIT):
    return pltpu.CompilerParams(dimension_semantics=("arbitrary",) * n_axes,
                                vmem_limit_bytes=vmem)


def _resident(shape):
    nd = len(shape)
    return pl.BlockSpec(shape, lambda *_: (0,) * nd, pipeline_mode=pl.Buffered(1))


def _dot(a, b):
    return jnp.dot(a, b, preferred_element_type=F32)


def _dot_nt(a, b):
    return lax.dot_general(a, b, (((1,), (1,)), ((), ())), preferred_element_type=F32)


def _dot_tn(a, b):
    return lax.dot_general(a, b, (((0,), (0,)), ((), ())), preferred_element_type=F32)


def _rms(x, g):
    return x * lax.rsqrt(jnp.mean(x * x, axis=-1, keepdims=True) + NORM_EPS) * g


def _ffn_kernel(*refs, has_mix, has_post, has_hout, bounds, d_ff):
    it = iter(refs)
    x_ref = next(it)
    if has_mix:
        o_ref, wo_ref = next(it), next(it)
    g_ref, win_ref, wout_ref = next(it), next(it), next(it)
    xo_ref = next(it)
    ho_ref = next(it) if has_hout else None

    x = x_ref[...]
    if has_mix:
        x = x + _dot(o_ref[...], wo_ref[...])
    h = _rms(x, g_ref[0:1, :]).astype(BF16)
    y = None
    for a, b in bounds:
        gate = _dot(h, win_ref[:, a:b])
        up = _dot(h, win_ref[:, d_ff + a:d_ff + b])
        act = (gate * jax.nn.sigmoid(gate) * up).astype(BF16)
        part = _dot(act, wout_ref[a:b, :])
        y = part if y is None else y + part
    x = x + 0.5 * y
    if has_post:
        x = _rms(x, g_ref[1:2, :])
    xo_ref[...] = x
    if has_hout:
        ho_ref[...] = _rms(x, g_ref[2:3, :]).astype(BF16)


def _ffn_bounds(d_ff):
    step = 4 * MXU_DIM
    return tuple((a, min(a + step, d_ff)) for a in range(0, d_ff, step))


def _ffn(x, gains, w_in, w_out, *, mix=None, has_post, has_hout, tm):
    t, d = x.shape
    d_ff = w_out.shape[0]
    tm = min(tm, t)
    row = lambda i: (i, 0)
    in_specs = [pl.BlockSpec((tm, d), row)]
    args = [x]
    if mix is not None:
        o, w_o = mix
        in_specs += [pl.BlockSpec((tm, o.shape[1]), row), _resident(w_o.shape)]
        args += [o, w_o]
    in_specs += [_resident(gains.shape), _resident(w_in.shape), _resident(w_out.shape)]
    args += [gains, w_in, w_out]
    out_shape = [jax.ShapeDtypeStruct((t, d), F32)]
    out_specs = [pl.BlockSpec((tm, d), row)]
    if has_hout:
        out_shape.append(jax.ShapeDtypeStruct((t, d), BF16))
        out_specs.append(pl.BlockSpec((tm, d), row))
    kern = functools.partial(_ffn_kernel, has_mix=mix is not None, has_post=has_post,
                             has_hout=has_hout, bounds=_ffn_bounds(d_ff), d_ff=d_ff)
    res = pl.pallas_call(kern, grid=(t // tm,), in_specs=in_specs, out_specs=out_specs,
                         out_shape=out_shape, compiler_params=_params(1))(*args)
    return res if has_hout else res[0]


def _head_mean_square(x, e_ref, head_dim):
    sq = (x * x).astype(BF16)
    parts = [_dot(sq[:, c:c + MXU_DIM], e_ref[...]) for c in range(0, x.shape[1], MXU_DIM)]
    return jnp.concatenate(parts, axis=1) * (1.0 / head_dim)


def _log_sigmoid(x):
    return jnp.minimum(x, 0.0) - jnp.log1p(jnp.exp(-jnp.abs(x)))


def _qkv_kernel(*refs, d, head_dim, has_f):
    it = iter(refs)
    h_ref, w_ref, e_ref, gq_ref, gk_ref = (next(it) for _ in range(5))
    if has_f:
        wf_ref, wft_ref, bf_ref, bft_ref = (next(it) for _ in range(4))
    q_ref, k_ref, v_ref = next(it), next(it), next(it)
    if has_f:
        lf_ref, lft_ref = next(it), next(it)

    h = h_ref[...]
    q = _dot(h, w_ref[:, 0:d])
    q = q * lax.rsqrt(_head_mean_square(q, e_ref, head_dim) + NORM_EPS) * gq_ref[...]
    q_ref[...] = q.astype(BF16)
    k = _dot(h, w_ref[:, d:2 * d])
    k_ref[...] = k * lax.rsqrt(_head_mean_square(k, e_ref, head_dim) + NORM_EPS) * gk_ref[...]
    v_ref[...] = _dot(h, w_ref[:, 2 * d:3 * d])
    if has_f:
        lf_ref[...] = _log_sigmoid(_dot(h, wf_ref[...]) + bf_ref[...])
        lft_ref[...] = _log_sigmoid(_dot_nt(wft_ref[...], h) + bft_ref[...])


def _qkv_proj(h, w, g_q, g_k, *, heads, forget=None, tm):
    t, d = h.shape
    tm = min(tm, t)
    head_dim = d // heads
    scale = head_dim ** -0.5
    blk = np.kron(np.eye(MXU_DIM // head_dim), np.ones((head_dim, head_dim)))
    e = jnp.asarray(blk, BF16)
    gq = (jnp.tile(g_q.astype(F32), heads) * scale).reshape(1, d)
    gk = jnp.tile(g_k.astype(F32), heads).reshape(1, d)
    row = lambda i: (i, 0)
    in_specs = [pl.BlockSpec((tm, d), row), _resident((d, 3 * d)), _resident(e.shape),
                _resident((1, d)), _resident((1, d))]
    args = [h, w[:, :3 * d].astype(BF16), e, gq, gk]
    out_shape = [jax.ShapeDtypeStruct((t, d), BF16), jax.ShapeDtypeStruct((t, d), F32),
                 jax.ShapeDtypeStruct((t, d), F32)]
    out_specs = [pl.BlockSpec((tm, d), row)] * 3
    if forget is not None:
        w_f, bias_f = forget
        args += [w_f.astype(BF16), w_f.T.astype(BF16), bias_f.reshape(1, heads).astype(F32),
                 bias_f.reshape(heads, 1).astype(F32)]
        in_specs += [_resident((d, heads)), _resident((heads, d)), _resident((1, heads)),
                     _resident((heads, 1))]
        out_shape += [jax.ShapeDtypeStruct((t, heads), F32), jax.ShapeDtypeStruct((heads, t), F32)]
        out_specs += [pl.BlockSpec((tm, heads), row), pl.BlockSpec((heads, tm), lambda i: (0, i))]
    kern = functools.partial(_qkv_kernel, d=d, head_dim=head_dim, has_f=forget is not None)
    return pl.pallas_call(kern, grid=(t // tm,), in_specs=in_specs, out_specs=out_specs,
                          out_shape=out_shape, compiler_params=_params(1))(*args)


def _head_masks(shape):
    lane = lax.broadcasted_iota(jnp.int32, shape, len(shape) - 1)
    first = lane < (HEAD_PAIR // 2)
    return first, jnp.logical_not(first)


def _softmax_pv(score_parts, value_parts):
    m = None
    for s in score_parts:
        mx = jnp.max(s, axis=-1, keepdims=True)
        m = mx if m is None else jnp.maximum(m, mx)
    acc, l = None, None
    for s, v in zip(score_parts, value_parts):
        p = jnp.exp(s - m)
        ls = jnp.sum(p, axis=-1, keepdims=True)
        pv = _dot(p.astype(BF16), v)
        acc = pv if acc is None else acc + pv
        l = ls if l is None else l + ls
    return acc * (1.0 / l)


def _band_prompt_kernel(q_ref, k_ref, v_ref, bm_ref, o_ref, kb_ref, vb_ref, *, tq, window):
    s_len = q_ref.shape[0]
    kb_ref[...] = k_ref[...].astype(BF16)
    vb_ref[...] = v_ref[...].astype(BF16)
    first, second = _head_masks((tq, HEAD_PAIR))
    width = window + tq
    for t in range(s_len // tq):
        lo, hi = max(0, t * tq - window), (t + 1) * tq
        q = q_ref[t * tq:hi, :]
        kw, vw = kb_ref[lo:hi, :], vb_ref[lo:hi, :]
        outs = []
        for hh, msk in enumerate((first, second)):
            qm = jnp.where(msk, q, jnp.zeros_like(q))
            s = _dot_nt(qm, kw) + bm_ref[hh, :, width - (hi - lo):]
            outs.append(_softmax_pv([s], [vw]))
        o_ref[t * tq:hi, :] = jnp.where(first, outs[0], outs[1]).astype(BF16)


def _band_bias_mask(rel_table, tq):
    width = A_WINDOW + tq
    i = np.arange(tq)[:, None]
    j = np.arange(width)[None, :]
    jj = j - CHUNK * (i // CHUNK)
    valid = (jj >= 0) & (jj < A_BAND)
    rel = (A_BAND - CHUNK) + (i % CHUNK) - jj
    idx = np.clip(rel, -A_REL_CLIP, A_REL_CLIP) + A_REL_CLIP
    bias = rel_table.astype(F32)[:, idx]
    return jnp.where(jnp.asarray(valid)[None], bias, NEG_INF)


def _band_prompt(q, k, v, rel_table, *, batch, tq):
    t, d = q.shape
    s_len = t // batch
    tq = min(tq, s_len)
    bm = _band_bias_mask(rel_table, tq)
    blk = pl.BlockSpec((s_len, HEAD_PAIR), lambda b, hp: (b, hp))
    kern = functools.partial(_band_prompt_kernel, tq=tq, window=A_WINDOW)
    return pl.pallas_call(
        kern, grid=(batch, d // HEAD_PAIR),
        in_specs=[blk, blk, blk, pl.BlockSpec((2,) + bm.shape[1:], lambda b, hp: (hp, 0, 0))],
        out_specs=blk, out_shape=jax.ShapeDtypeStruct((t, d), BF16),
        scratch_shapes=[pltpu.VMEM((s_len, HEAD_PAIR), BF16)] * 2,
        compiler_params=_params(2))(q, k, v, bm)


def _band_sample_kernel(q_ref, k_ref, v_ref, ck_ref, cv_ref, bias_ref, o_ref):
    n_new, d = q_ref.shape
    win = ck_ref.shape[1]
    first, second = _head_masks((n_new, HEAD_PAIR))
    for hp in range(d // HEAD_PAIR):
        cols = slice(hp * HEAD_PAIR, (hp + 1) * HEAD_PAIR)
        q = q_ref[:, cols]
        kc, vc = ck_ref[0, :, cols].astype(BF16), cv_ref[0, :, cols].astype(BF16)
        kn, vn = k_ref[:, cols].astype(BF16), v_ref[:, cols].astype(BF16)
        outs = []
        for hh, msk in enumerate((first, second)):
            qm = jnp.where(msk, q, jnp.zeros_like(q))
            bias = bias_ref[2 * hp + hh]
            s_c = _dot_nt(qm, kc) + bias[:, :win]
            s_n = _dot_nt(qm, kn) + bias[:, win:]
            outs.append(_softmax_pv([s_c, s_n], [vc, vn]))
        o_ref[:, cols] = jnp.where(first, outs[0], outs[1]).astype(BF16)


def _band_sample(q, k, v, cache_k, cache_v, rel_table, *, batch):
    t, d = q.shape
    n_new = t // batch
    win = cache_k.shape[1]
    rel = win + np.arange(n_new)[:, None] - np.arange(win + n_new)[None, :]
    idx = np.clip(rel, -A_REL_CLIP, A_REL_CLIP) + A_REL_CLIP
    bias = rel_table.astype(F32)[:, idx]
    row = pl.BlockSpec((n_new, d), lambda b: (b, 0))
    cache = pl.BlockSpec((1, win, d), lambda b: (b, 0, 0))
    return pl.pallas_call(
        _band_sample_kernel, grid=(batch,),
        in_specs=[row, row, row, cache, cache, _resident(bias.shape)],
        out_specs=row, out_shape=jax.ShapeDtypeStruct((t, d), BF16),
        compiler_params=_params(1))(q, k, v, cache_k.reshape(batch, win, d),
                                    cache_v.reshape(batch, win, d), bias)


def _lane_cumsum(x):
    n = x.shape[-1]
    lane = lax.broadcasted_iota(jnp.int32, x.shape, x.ndim - 1)
    shift = 1
    while shift < n:
        x = x + jnp.where(lane >= shift, pltpu.roll(x, shift, x.ndim - 1), 0.0)
        shift *= 2
    return x


def _fox_prompt_kernel(q_ref, k_ref, v_ref, lft_ref, o_ref, kb_ref, vb_ref, *, tq):
    s_len = q_ref.shape[0]
    kb_ref[...] = k_ref[...].astype(BF16)
    vb_ref[...] = v_ref[...].astype(BF16)
    cum = _lane_cumsum(lft_ref[0])
    first, second = _head_masks((tq, HEAD_PAIR))
    r = lax.broadcasted_iota(jnp.int32, (tq, tq), 0)
    c = lax.broadcasted_iota(jnp.int32, (tq, tq), 1)
    causal = c <= r
    for t in range(s_len // tq):
        lo, hi = t * tq, (t + 1) * tq
        q = q_ref[lo:hi, :]
        outs = []
        for hh, msk in enumerate((first, second)):
            qm = jnp.where(msk, q, jnp.zeros_like(q))
            s_d = _dot_nt(qm, kb_ref[lo:hi, :]) - cum[hh:hh + 1, lo:hi]
            scores, values = [jnp.where(causal, s_d, NEG_INF)], [vb_ref[lo:hi, :]]
            if t > 0:
                scores.append(_dot_nt(qm, kb_ref[0:lo, :]) - cum[hh:hh + 1, 0:lo])
                values.append(vb_ref[0:lo, :])
            outs.append(_softmax_pv(scores, values))
        o_ref[lo:hi, :] = jnp.where(first, outs[0], outs[1]).astype(BF16)


def _fox_prompt(q, k, v, logf_t, *, batch, tq):
    t, d = q.shape
    s_len = t // batch
    tq = min(tq, s_len)
    heads = logf_t.shape[0]
    blk = pl.BlockSpec((s_len, HEAD_PAIR), lambda b, hp: (b, hp))
    kern = functools.partial(_fox_prompt_kernel, tq=tq)
    return pl.pallas_call(
        kern, grid=(batch, d // HEAD_PAIR),
        in_specs=[blk, blk, blk, pl.BlockSpec((1, 2, s_len), lambda b, hp: (hp, 0, b))],
        out_specs=blk, out_shape=jax.ShapeDtypeStruct((t, d), BF16),
        scratch_shapes=[pltpu.VMEM((s_len, HEAD_PAIR), BF16)] * 2,
        compiler_params=_params(2))(q, k, v, logf_t.reshape(heads // 2, 2, t))


def _fox_sample_kernel(q_ref, k_ref, v_ref, ck_ref, cv_ref, lft_ref, o_ref, *, past):
    n_new = q_ref.shape[0]
    cum = _lane_cumsum(lft_ref[0, 0])
    first, second = _head_masks((n_new, HEAD_PAIR))
    r = lax.broadcasted_iota(jnp.int32, (n_new, n_new), 0)
    c = lax.broadcasted_iota(jnp.int32, (n_new, n_new), 1)
    causal = c <= r
    q = q_ref[...]
    kc, vc = ck_ref[0].astype(BF16), cv_ref[0].astype(BF16)
    kn, vn = k_ref[...].astype(BF16), v_ref[...].astype(BF16)
    outs = []
    for hh, msk in enumerate((first, second)):
        qm = jnp.where(msk, q, jnp.zeros_like(q))
        s_c = _dot_nt(qm, kc) - cum[hh:hh + 1, 0:past]
        s_n = _dot_nt(qm, kn) - cum[hh:hh + 1, past:past + n_new]
        s_n = jnp.where(causal, s_n, NEG_INF)
        outs.append(_softmax_pv([s_c, s_n], [vc, vn]))
    o_ref[...] = jnp.where(first, outs[0], outs[1]).astype(BF16)


def _fox_sample(q, k, v, cache_k, cache_v, logf_all_t, *, batch, past):
    t, d = q.shape
    n_new = t // batch
    padded = logf_all_t.shape[-1]
    row = pl.BlockSpec((n_new, HEAD_PAIR), lambda b, hp: (b, hp))
    cache = pl.BlockSpec((1, past, HEAD_PAIR), lambda b, hp: (b, 0, hp))
    kern = functools.partial(_fox_sample_kernel, past=past)
    return pl.pallas_call(
        kern, grid=(batch, d // HEAD_PAIR),
        in_specs=[row, row, row, cache, cache,
                  pl.BlockSpec((1, 1, 2, padded), lambda b, hp: (b, hp, 0, 0))],
        out_specs=row, out_shape=jax.ShapeDtypeStruct((t, d), BF16),
        compiler_params=_params(2))(q, k, v, cache_k.reshape(batch, past, d),
                                    cache_v.reshape(batch, past, d), logf_all_t)


RET_BLOCK = 256


def _ret_proj_kernel(h_ref, w_ref, cos_ref, sin_ref, q_ref, k_ref, v_ref, g_ref, *,
                     heads, key_dim, val_width):
    h = h_ref[...]
    cos, sin = cos_ref[...], sin_ref[...]
    half = key_dim // 2
    qk_width = heads * key_dim
    k_scale = key_dim ** -0.5
    for part, (ref, scale) in enumerate(((q_ref, 1.0), (k_ref, k_scale))):
        for hd in range(heads):
            c0 = part * qk_width + hd * key_dim
            x1 = _dot(h, w_ref[:, c0:c0 + half])
            x2 = _dot(h, w_ref[:, c0 + half:c0 + key_dim])
            o0 = hd * key_dim
            ref[:, o0:o0 + half] = ((x1 * cos - x2 * sin) * scale).astype(BF16)
            ref[:, o0 + half:o0 + key_dim] = ((x2 * cos + x1 * sin) * scale).astype(BF16)
    v0 = 2 * qk_width
    step = 4 * MXU_DIM
    for c in range(0, val_width, step):
        v_ref[:, c:c + step] = _dot(h, w_ref[:, v0 + c:v0 + c + step]).astype(BF16)
        g = _dot(h, w_ref[:, v0 + val_width + c:v0 + val_width + c + step])
        g_ref[:, c:c + step] = (g * jax.nn.sigmoid(g)).astype(BF16)


def _ret_proj(h, w, cos, sin, *, heads, tm):
    t, d = h.shape
    tm = min(tm, t, cos.shape[0])
    key_dim = d // heads
    val_width = 2 * d
    n_period = cos.shape[0] // tm
    row = lambda i: (i, 0)
    tab = pl.BlockSpec((tm, key_dim // 2), lambda i: (i % n_period, 0))
    kern = functools.partial(_ret_proj_kernel, heads=heads, key_dim=key_dim, val_width=val_width)
    return pl.pallas_call(
        kern, grid=(t // tm,),
        in_specs=[pl.BlockSpec((tm, d), row), _resident(w.shape), tab, tab],
        out_specs=[pl.BlockSpec((tm, d), row), pl.BlockSpec((tm, d), row),
                   pl.BlockSpec((tm, val_width), row), pl.BlockSpec((tm, val_width), row)],
        out_shape=[jax.ShapeDtypeStruct((t, d), BF16), jax.ShapeDtypeStruct((t, d), BF16),
                   jax.ShapeDtypeStruct((t, val_width), BF16),
                   jax.ShapeDtypeStruct((t, val_width), BF16)],
        compiler_params=_params(1))(h, w.astype(BF16), cos, sin)


def _retention_kernel(*refs, n_blocks, blk, has_init):
    it = iter(refs)
    q_ref, k_ref, v_ref, g_ref, gn_ref = (next(it) for _ in range(5))
    dec_ref, qd_ref, kd_ref, sd_ref = (next(it) for _ in range(4))
    s0_ref = next(it) if has_init else None
    y_ref, so_ref, st_ref = next(it), next(it), next(it)

    if has_init:
        st_ref[...] = s0_ref[0, 0]
    else:
        st_ref[...] = jnp.zeros_like(st_ref)
    decay = dec_ref[0]
    q_decay, k_decay, s_decay = qd_ref[0], kd_ref[0], sd_ref[0]
    for c in range(n_blocks):
        rows = slice(c * blk, (c + 1) * blk)
        q, k, v = q_ref[rows, :], k_ref[rows, :], v_ref[rows, :]
        state = st_ref[...]
        scores = (_dot_nt(q, k) * decay).astype(BF16)
        o = _dot(scores, v) + _dot(q, state.astype(BF16)) * q_decay
        k_dec = (k.astype(F32) * k_decay).astype(BF16)
        st_ref[...] = s_decay * state + _dot_tn(k_dec, v)
        mu = jnp.mean(o, axis=-1, keepdims=True)
        cen = o - mu
        var = jnp.mean(cen * cen, axis=-1, keepdims=True)
        y = g_ref[rows, :].astype(F32) * (cen * lax.rsqrt(var + GN_EPS) * gn_ref[...])
        y_ref[rows, :] = y.astype(BF16)
    so_ref[0, 0] = st_ref[...]


def _retention_decays(heads, blk):
    log_gamma = np.log1p(-np.exp2(-5.0 - np.arange(heads, dtype=np.float64)))
    idx = np.arange(blk, dtype=np.float64)
    diff = idx[:, None] - idx[None, :]
    decay = np.where(diff >= 0, np.exp(np.maximum(diff, 0.0)[None] * log_gamma[:, None, None]), 0.0)
    q_decay = np.exp((idx + 1.0)[None, :] * log_gamma[:, None])[..., None]
    k_decay = np.exp((blk - 1.0 - idx)[None, :] * log_gamma[:, None])[..., None]
    s_decay = np.exp(blk * log_gamma)[:, None, None]
    return tuple(jnp.asarray(a, F32) for a in (decay, q_decay, k_decay, s_decay))


def _retention(q, k, v, gate, gn_g, state0, *, batch, heads):
    t, d = q.shape
    s_len = t // batch
    key_dim = d // heads
    val_dim = v.shape[1] // heads
    blk = min(RET_BLOCK, s_len)
    decay, q_decay, k_decay, s_decay = _retention_decays(heads, blk)
    qk_blk = pl.BlockSpec((s_len, key_dim), lambda b, h: (b, h))
    v_blk = pl.BlockSpec((s_len, val_dim), lambda b, h: (b, h))
    per_head = lambda shape: pl.BlockSpec((1,) + shape, lambda b, h: (h, 0, 0))
    state_blk = pl.BlockSpec((1, 1, key_dim, val_dim), lambda b, h: (b, h, 0, 0))
    in_specs = [qk_blk, qk_blk, v_blk, v_blk, pl.BlockSpec((1, val_dim), lambda b, h: (0, h)),
                per_head((blk, blk)), per_head((blk, 1)), per_head((blk, 1)), per_head((1, 1))]
    args = [q, k, v, gate, gn_g.reshape(1, -1).astype(F32), decay, q_decay, k_decay, s_decay]
    if state0 is not None:
        in_specs.append(state_blk)
        args.append(state0)
    kern = functools.partial(_retention_kernel, n_blocks=s_len // blk, blk=blk,
                             has_init=state0 is not None)
    return pl.pallas_call(
        kern, grid=(batch, heads), in_specs=in_specs, out_specs=[v_blk, state_blk],
        out_shape=[jax.ShapeDtypeStruct(v.shape, BF16),
                   jax.ShapeDtypeStruct((batch, heads, key_dim, val_dim), F32)],
        scratch_shapes=[pltpu.VMEM((key_dim, val_dim), F32)],
        compiler_params=_params(2))(*args)


def _rotary_tables(pos, half):
    inv_freq = ROPE_BASE ** (-jnp.arange(half, dtype=F32) / half)
    ang = pos.astype(F32)[:, None] * inv_freq[None, :]
    return jnp.cos(ang), jnp.sin(ang)


TM_PROMPT = 512


def kernel(x_prompt, x_sample, cache_chunk_k, cache_chunk_v, cache_fox_k, cache_fox_v, cache_fox_logf,
           state_ret, norm_g, w_ffn_in, w_ffn_out, a_w_in, a_g_q, a_g_k, a_rel_table, a_w_out,
           b_w_in, b_bias_f, b_g_q, b_g_k, b_w_out, c_w_in, c_gn_g, c_w_out):
    batch, seq, d = x_prompt.shape
    dec_batch, dec_seq, _ = x_sample.shape
    past = cache_fox_k.shape[2]
    depth = norm_g.shape[0]
    xp = x_prompt.reshape(batch * seq, d)
    xs = x_sample.reshape(dec_batch * dec_seq, d)
    streams = ((batch, seq), (dec_batch, dec_seq))
    head_dim = d // A_HEADS
    outs = {name: [] for name in ("a_kp", "a_vp", "a_ks", "a_vs", "b_kp", "b_vp", "b_fp",
                                  "b_ks", "b_vs", "b_fs", "c_sp", "c_ss")}

    for i in range(depth):
        g = norm_g[i].astype(F32)
        kind, j = i % N_MIXERS, i // N_MIXERS
        w1_in, w1_out = w_ffn_in[i, 0].astype(BF16), w_ffn_out[i, 0].astype(BF16)
        w2_in, w2_out = w_ffn_in[i, 1].astype(BF16), w_ffn_out[i, 1].astype(BF16)
        gains1 = jnp.stack([g[0], g[0], g[1]])
        gains2 = jnp.stack([g[2], g[3], g[3]])
        xp, hp = _ffn(xp, gains1, w1_in, w1_out, has_post=False, has_hout=True, tm=TM_PROMPT)
        xs, hs = _ffn(xs, gains1, w1_in, w1_out, has_post=False, has_hout=True, tm=TM_PROMPT)

        if kind == 0:
            qp, kp, vp = _qkv_proj(hp, a_w_in[j], a_g_q[j], a_g_k[j], heads=A_HEADS, tm=TM_PROMPT)
            qs, ks, vs = _qkv_proj(hs, a_w_in[j], a_g_q[j], a_g_k[j], heads=A_HEADS, tm=TM_PROMPT)
            op = _band_prompt(qp, kp, vp, a_rel_table[j], batch=batch, tq=256)
            os_ = _band_sample(qs, ks, vs, cache_chunk_k[j], cache_chunk_v[j], a_rel_table[j],
                               batch=dec_batch)
            keep = min(A_WINDOW, seq)
            outs["a_kp"].append(kp.reshape(batch, seq, A_HEADS, head_dim)[:, seq - keep:])
            outs["a_vp"].append(vp.reshape(batch, seq, A_HEADS, head_dim)[:, seq - keep:])
            outs["a_ks"].append(ks.reshape(dec_batch, dec_seq, A_HEADS, head_dim))
            outs["a_vs"].append(vs.reshape(dec_batch, dec_seq, A_HEADS, head_dim))
            w_o = a_w_out[j].astype(BF16)
        elif kind == 1:
            forget = (b_w_in[j][:, 3 * d:], b_bias_f[j])
            qp, kp, vp, fp, fpt = _qkv_proj(hp, b_w_in[j], b_g_q[j], b_g_k[j], heads=B_HEADS,
                                            forget=forget, tm=TM_PROMPT)
            qs, ks, vs, fs, _ = _qkv_proj(hs, b_w_in[j], b_g_q[j], b_g_k[j], heads=B_HEADS,
                                          forget=forget, tm=TM_PROMPT)
            op = _fox_prompt(qp, kp, vp, fpt, batch=batch, tq=256)
            total = past + dec_seq
            padded = -(-total // LANES) * LANES
            lf_all = jnp.concatenate([cache_fox_logf[j].astype(F32),
                                      fs.reshape(dec_batch, dec_seq, B_HEADS)], axis=1)
            lf_all = jnp.pad(jnp.swapaxes(lf_all, 1, 2), ((0, 0), (0, 0), (0, padded - total)))
            os_ = _fox_sample(qs, ks, vs, cache_fox_k[j], cache_fox_v[j],
                              lf_all.reshape(dec_batch, B_HEADS // 2, 2, padded),
                              batch=dec_batch, past=past)
            outs["b_kp"].append(kp.reshape(batch, seq, B_HEADS, head_dim))
            outs["b_vp"].append(vp.reshape(batch, seq, B_HEADS, head_dim))
            outs["b_fp"].append(fp.reshape(batch, seq, B_HEADS))
            outs["b_ks"].append(ks.reshape(dec_batch, dec_seq, B_HEADS, head_dim))
            outs["b_vs"].append(vs.reshape(dec_batch, dec_seq, B_HEADS, head_dim))
            outs["b_fs"].append(fs.reshape(dec_batch, dec_seq, B_HEADS))
            w_o = b_w_out[j].astype(BF16)
        else:
            half = d // C_HEADS // 2
            cos_p, sin_p = _rotary_tables(jnp.arange(seq), half)
            cos_s, sin_s = _rotary_tables(past + jnp.arange(dec_seq), half)
            cos_s, sin_s = jnp.tile(cos_s, (dec_batch, 1)), jnp.tile(sin_s, (dec_batch, 1))
            qp, kp, vp, gp = _ret_proj(hp, c_w_in[j], cos_p, sin_p, heads=C_HEADS, tm=TM_PROMPT)
            qs, ks, vs, gs = _ret_proj(hs, c_w_in[j], cos_s, sin_s, heads=C_HEADS, tm=TM_PROMPT)
            op, sp = _retention(qp, kp, vp, gp, c_gn_g[j], None, batch=batch, heads=C_HEADS)
            os_, ss = _retention(qs, ks, vs, gs, c_gn_g[j], state_ret[j].astype(F32),
                                 batch=dec_batch, heads=C_HEADS)
            outs["c_sp"].append(sp)
            outs["c_ss"].append(ss)
            w_o = c_w_out[j].astype(BF16)

        xp = _ffn(xp, gains2, w2_in, w2_out, mix=(op, w_o), has_post=True, has_hout=False,
                  tm=TM_PROMPT)
        xs = _ffn(xs, gains2, w2_in, w2_out, mix=(os_, w_o), has_post=True, has_hout=False,
                  tm=TM_PROMPT)

    stack = lambda name: jnp.stack(outs[name])
    return (xp.reshape(batch, seq, d), xs.reshape(dec_batch, dec_seq, d),
            stack("a_kp"), stack("a_vp"), stack("a_ks"), stack("a_vs"),
            stack("b_kp"), stack("b_vp"), stack("b_fp"), stack("b_ks"), stack("b_vs"), stack("b_fs"),
            stack("c_sp"), stack("c_ss"))
```

```python
import functools

import numpy as np
import jax
import jax.numpy as jnp
from jax import lax
from jax.experimental import pallas as pl
from jax.experimental.pallas import tpu as pltpu

CHUNK = 64
A_HEADS = 16
A_LEFT_CHUNKS = 8
A_WINDOW = A_LEFT_CHUNKS * CHUNK
A_BAND = A_WINDOW + CHUNK
A_REL_CLIP = 128
B_HEADS = 16
C_HEADS = 4
ROPE_BASE = 10000.0
NORM_EPS = 1e-6
GN_EPS = 1e-5
NEG_INF = -1e30
N_MIXERS = 3
LOG2E = 1.4426950408889634

LANES = 128
HEAD_PAIR = LANES
MXU_DIM = 256
VMEM_LIMIT = 56 * 1024 * 1024

BF16 = jnp.bfloat16
F32 = jnp.float32


def _params(n_axes, vmem=VMEM_LIMIT):
    return pltpu.CompilerParams(dimension_semantics=("arbitrary",) * n_axes,
                                vmem_limit_bytes=vmem)


def _resident(shape):
    nd = len(shape)
    return pl.BlockSpec(shape, lambda *_: (0,) * nd, pipeline_mode=pl.Buffered(1))


def _dot(a, b):
    return jnp.dot(a, b, preferred_element_type=F32)


def _dot_nt(a, b):
    return lax.dot_general(a, b, (((1,), (1,)), ((), ())), preferred_element_type=F32)


def _dot_tn(a, b):
    return lax.dot_general(a, b, (((0,), (0,)), ((), ())), preferred_element_type=F32)


def _rms(x, g):
    return x * lax.rsqrt(jnp.mean(x * x, axis=-1, keepdims=True) + NORM_EPS) * g


def _ffn_kernel(*refs, has_mix, has_post, has_hout, bounds, d_ff):
    it = iter(refs)
    x_ref = next(it)
    if has_mix:
        o_ref, wo_ref = next(it), next(it)
    g_ref, win_ref, wout_ref = next(it), next(it), next(it)
    xo_ref = next(it)
    ho_ref = next(it) if has_hout else None

    x = x_ref[...]
    if has_mix:
        x = x + _dot(o_ref[...], wo_ref[...])
    h = _rms(x, g_ref[0:1, :]).astype(BF16)
    y = None
    for a, b in bounds:
        gate = _dot(h, win_ref[:, a:b])
        up = _dot(h, win_ref[:, d_ff + a:d_ff + b])
        act = (gate * jax.nn.sigmoid(gate) * up).astype(BF16)
        part = _dot(act, wout_ref[a:b, :])
        y = part if y is None else y + part
    x = x + 0.5 * y
    if has_post:
        x = _rms(x, g_ref[1:2, :])
    xo_ref[...] = x
    if has_hout:
        ho_ref[...] = _rms(x, g_ref[2:3, :]).astype(BF16)


def _ffn_bounds(d_ff):
    step = 4 * MXU_DIM
    return tuple((a, min(a + step, d_ff)) for a in range(0, d_ff, step))


def _ffn(x, gains, w_in, w_out, *, mix=None, has_post, has_hout, tm):
    t, d = x.shape
    d_ff = w_out.shape[0]
    tm = min(tm, t)
    row = lambda i: (i, 0)
    in_specs = [pl.BlockSpec((tm, d), row)]
    args = [x]
    if mix is not None:
        o, w_o = mix
        in_specs += [pl.BlockSpec((tm, o.shape[1]), row), _resident(w_o.shape)]
        args += [o, w_o]
    in_specs += [_resident(gains.shape), _resident(w_in.shape), _resident(w_out.shape)]
    args += [gains, w_in, w_out]
    out_shape = [jax.ShapeDtypeStruct((t, d), F32)]
    out_specs = [pl.BlockSpec((tm, d), row)]
    if has_hout:
        out_shape.append(jax.ShapeDtypeStruct((t, d), BF16))
        out_specs.append(pl.BlockSpec((tm, d), row))
    kern = functools.partial(_ffn_kernel, has_mix=mix is not None, has_post=has_post,
                             has_hout=has_hout, bounds=_ffn_bounds(d_ff), d_ff=d_ff)
    res = pl.pallas_call(kern, grid=(t // tm,), in_specs=in_specs, out_specs=out_specs,
                         out_shape=out_shape, compiler_params=_params(1))(*args)
    return res if has_hout else res[0]


def _head_mean_square(x, e_ref, head_dim):
    sq = (x * x).astype(BF16)
    parts = [_dot(sq[:, c:c + MXU_DIM], e_ref[...]) for c in range(0, x.shape[1], MXU_DIM)]
    return jnp.concatenate(parts, axis=1) * (1.0 / head_dim)


def _log_sigmoid(x):
    return jnp.minimum(x, 0.0) - jnp.log1p(jnp.exp(-jnp.abs(x)))


def _qkv_kernel(*refs, d, head_dim, has_f, tail_tiles):
    it = iter(refs)
    h_ref, w_ref, e_ref, gq_ref, gk_ref = (next(it) for _ in range(5))
    if has_f:
        wf_ref, wft_ref, bf_ref, bft_ref = (next(it) for _ in range(4))
    q_ref, kb_ref, vb_ref, kf_ref, vf_ref = (next(it) for _ in range(5))
    if has_f:
        lf_ref, lft_ref = next(it), next(it)

    h = h_ref[...]
    q = _dot(h, w_ref[:, 0:d])
    q = q * lax.rsqrt(_head_mean_square(q, e_ref, head_dim) + NORM_EPS) * gq_ref[...]
    q_ref[...] = q.astype(BF16)
    k = _dot(h, w_ref[:, d:2 * d])
    k = k * lax.rsqrt(_head_mean_square(k, e_ref, head_dim) + NORM_EPS) * gk_ref[...]
    v = _dot(h, w_ref[:, 2 * d:3 * d])
    kb_ref[...] = k.astype(BF16)
    vb_ref[...] = v.astype(BF16)

    def write_f32():
        heads, rows = d // head_dim, h.shape[0]
        for hd in range(heads):
            cols = slice(hd * head_dim, (hd + 1) * head_dim)
            kf_ref[pl.ds(hd, rows, stride=heads), :] = k[:, cols]
            vf_ref[pl.ds(hd, rows, stride=heads), :] = v[:, cols]

    if tail_tiles is None:
        write_f32()
    else:
        per_seq, kept = tail_tiles
        pl.when(pl.program_id(0) % per_seq >= per_seq - kept)(write_f32)
    if has_f:
        lf_ref[...] = _log_sigmoid(_dot(h, wf_ref[...]) + bf_ref[...])
        lft_ref[...] = _log_sigmoid(_dot_nt(wft_ref[...], h) + bft_ref[...])


def _qkv_proj(h, w, g_q, g_k, *, heads, forget=None, tail=None, tm):
    t, d = h.shape
    tm = min(tm, t)
    head_dim = d // heads
    scale = head_dim ** -0.5 * LOG2E
    blk = np.kron(np.eye(MXU_DIM // head_dim), np.ones((head_dim, head_dim)))
    e = jnp.asarray(blk, BF16)
    gq = (jnp.tile(g_q.astype(F32), heads) * scale).reshape(1, d)
    gk = jnp.tile(g_k.astype(F32), heads).reshape(1, d)
    row = lambda i: (i, 0)
    if tail is None or tail[0] == tail[1]:
        tail_tiles, f32_rows, f32_row = None, t, row
    else:
        seq, keep = tail
        assert seq % tm == 0 and keep % tm == 0
        per_seq, kept = seq // tm, keep // tm
        tail_tiles, f32_rows = (per_seq, kept), (t // seq) * keep
        f32_row = lambda i: ((i // per_seq) * kept + jnp.maximum(i % per_seq - (per_seq - kept), 0), 0)
    in_specs = [pl.BlockSpec((tm, d), row), _resident((d, 3 * d)), _resident(e.shape),
                _resident((1, d)), _resident((1, d))]
    args = [h, w[:, :3 * d].astype(BF16), e, gq, gk]
    out_shape = ([jax.ShapeDtypeStruct((t, d), BF16)] * 3
                 + [jax.ShapeDtypeStruct((f32_rows * heads, head_dim), F32)] * 2)
    out_specs = ([pl.BlockSpec((tm, d), row)] * 3
                 + [pl.BlockSpec((tm * heads, head_dim), f32_row)] * 2)
    if forget is not None:
        w_f, bias_f = forget
        args += [w_f.astype(BF16), w_f.T.astype(BF16), bias_f.reshape(1, heads).astype(F32),
                 bias_f.reshape(heads, 1).astype(F32)]
        in_specs += [_resident((d, heads)), _resident((heads, d)), _resident((1, heads)),
                     _resident((heads, 1))]
        out_shape += [jax.ShapeDtypeStruct((t, heads), F32), jax.ShapeDtypeStruct((heads, t), F32)]
        out_specs += [pl.BlockSpec((tm, heads), row), pl.BlockSpec((heads, tm), lambda i: (0, i))]
    kern = functools.partial(_qkv_kernel, d=d, head_dim=head_dim, has_f=forget is not None,
                             tail_tiles=tail_tiles)
    return pl.pallas_call(kern, grid=(t // tm,), in_specs=in_specs, out_specs=out_specs,
                          out_shape=out_shape, compiler_params=_params(1))(*args)


def _head_masks(shape):
    lane = lax.broadcasted_iota(jnp.int32, shape, len(shape) - 1)
    first = lane < (HEAD_PAIR // 2)
    return first, jnp.logical_not(first)


def _softmax_pv(score_parts, value_parts):
    m = None
    for s in score_parts:
        mx = jnp.max(s, axis=-1, keepdims=True)
        m = mx if m is None else jnp.maximum(m, mx)
    acc, l = None, None
    for s, v in zip(score_parts, value_parts):
        p = jnp.exp2(s - m)
        ls = jnp.sum(p, axis=-1, keepdims=True)
        pv = _dot(p.astype(BF16), v)
        acc = pv if acc is None else acc + pv
        l = ls if l is None else l + ls
    return acc * (1.0 / l)


def _stack_heads(q, first, second):
    zero = jnp.zeros_like(q)
    return jnp.concatenate([jnp.where(first, q, zero), jnp.where(second, q, zero)], axis=0)


def _pv_normalised(p, vw, first, tq):
    ones = jnp.ones(vw.shape, vw.dtype)
    o2 = _dot(p, jnp.concatenate([vw, ones], axis=1))
    o2 = o2[:, :HEAD_PAIR] * (1.0 / o2[:, HEAD_PAIR:])
    return jnp.where(first, o2[:tq], o2[tq:])


def _toeplitz_bias(rel_table, n_rows, n_cols, offset):
    n_diag = n_rows + n_cols - 1
    u = np.arange(n_diag) - (n_rows - 1)
    idx = np.clip(offset - u, -A_REL_CLIP, A_REL_CLIP) + A_REL_CLIP
    diag = jnp.pad(rel_table.astype(F32)[:, idx], ((0, 0), (0, 1)))
    heads = diag.shape[0]
    skew = jnp.tile(diag, (1, n_rows))[:, :n_rows * n_diag].reshape(heads, n_rows, n_diag)
    return skew[:, :, n_rows - 1:n_rows - 1 + n_cols]


def _band_prompt_kernel(q_ref, k_ref, v_ref, bm_ref, o_ref, *, tq, window):
    s_len = q_ref.shape[0]
    first, second = _head_masks((tq, HEAD_PAIR))
    width = window + tq
    for t in range(s_len // tq):
        lo, hi = max(0, t * tq - window), (t + 1) * tq
        qs = _stack_heads(q_ref[t * tq:hi, :], first, second)
        s = _dot_nt(qs, k_ref[lo:hi, :]) + bm_ref[0, :, width - (hi - lo):]
        p = jnp.exp2(s - jnp.max(s, axis=-1, keepdims=True)).astype(BF16)
        o_ref[t * tq:hi, :] = _pv_normalised(p, v_ref[lo:hi, :], first, tq).astype(BF16)


def _band_bias_mask(rel_table, tq):
    width = A_WINDOW + tq
    chunk_bias = _toeplitz_bias(rel_table, CHUNK, A_BAND, A_BAND - CHUNK) * LOG2E
    rows = [jnp.pad(chunk_bias, ((0, 0), (0, 0), (c * CHUNK, width - A_BAND - c * CHUNK)),
                    constant_values=NEG_INF) for c in range(tq // CHUNK)]
    bm = jnp.concatenate(rows, axis=1)
    return bm.reshape(bm.shape[0] // 2, 2 * tq, width)


def _band_prompt(q, k, v, rel_table, *, batch, tq):
    t, d = q.shape
    s_len = t // batch
    tq = min(tq, s_len)
    bm = _band_bias_mask(rel_table, tq)
    blk = pl.BlockSpec((s_len, HEAD_PAIR), lambda b, hp: (b, hp))
    kern = functools.partial(_band_prompt_kernel, tq=tq, window=A_WINDOW)
    return pl.pallas_call(
        kern, grid=(batch, d // HEAD_PAIR),
        in_specs=[blk, blk, blk, pl.BlockSpec((1,) + bm.shape[1:], lambda b, hp: (hp, 0, 0))],
        out_specs=blk, out_shape=jax.ShapeDtypeStruct((t, d), BF16),
        compiler_params=_params(2))(q, k, v, bm)


def _band_sample_kernel(q_ref, k_ref, v_ref, ck_ref, cv_ref, bias_ref, o_ref):
    n_new, d = q_ref.shape
    win = ck_ref.shape[1]
    first, second = _head_masks((n_new, HEAD_PAIR))
    for hp in range(d // HEAD_PAIR):
        cols = slice(hp * HEAD_PAIR, (hp + 1) * HEAD_PAIR)
        q = q_ref[:, cols]
        kc, vc = ck_ref[0, :, cols], cv_ref[0, :, cols]
        kn, vn = k_ref[:, cols], v_ref[:, cols]
        outs = []
        for hh, msk in enumerate((first, second)):
            qm = jnp.where(msk, q, jnp.zeros_like(q))
            bias = bias_ref[2 * hp + hh]
            s_c = _dot_nt(qm, kc) + bias[:, :win]
            s_n = _dot_nt(qm, kn) + bias[:, win:]
            outs.append(_softmax_pv([s_c, s_n], [vc, vn]))
        o_ref[:, cols] = jnp.where(first, outs[0], outs[1]).astype(BF16)


def _band_sample(q, k, v, cache_k, cache_v, rel_table, *, batch):
    t, d = q.shape
    n_new = t // batch
    win = cache_k.shape[1]
    bias = _toeplitz_bias(rel_table, n_new, win + n_new, win) * LOG2E
    row = pl.BlockSpec((n_new, d), lambda b: (b, 0))
    cache = pl.BlockSpec((1, win, d), lambda b: (b, 0, 0))
    return pl.pallas_call(
        _band_sample_kernel, grid=(batch,),
        in_specs=[row, row, row, cache, cache, _resident(bias.shape)],
        out_specs=row, out_shape=jax.ShapeDtypeStruct((t, d), BF16),
        compiler_params=_params(1))(q, k, v, cache_k.astype(BF16).reshape(batch, win, d),
                                    cache_v.astype(BF16).reshape(batch, win, d), bias)


def _lane_cumsum(x):
    n = x.shape[-1]
    lane = lax.broadcasted_iota(jnp.int32, x.shape, x.ndim - 1)
    shift = 1
    while shift < n:
        x = x + jnp.where(lane >= shift, pltpu.roll(x, shift, x.ndim - 1), 0.0)
        shift *= 2
    return x


def _fox_prompt_kernel(q_ref, k_ref, v_ref, lft_ref, o_ref, *, tq):
    s_len = q_ref.shape[0]
    cum = _lane_cumsum(lft_ref[0]) * LOG2E
    first, second = _head_masks((tq, HEAD_PAIR))
    r = lax.broadcasted_iota(jnp.int32, (2 * tq, tq), 0)
    c = lax.broadcasted_iota(jnp.int32, (2 * tq, tq), 1)
    causal = c <= jnp.where(r >= tq, r - tq, r)
    for t in range(s_len // tq):
        lo, hi = t * tq, (t + 1) * tq
        qs = _stack_heads(q_ref[lo:hi, :], first, second)
        s = _dot_nt(qs, k_ref[0:hi, :])
        s = jnp.concatenate([s[:tq] - cum[0:1, 0:hi], s[tq:] - cum[1:2, 0:hi]], axis=0)
        parts = [jnp.where(causal, s[:, lo:hi], NEG_INF)]
        m = jnp.max(parts[0], axis=-1, keepdims=True)
        if t > 0:
            parts.insert(0, s[:, 0:lo])
            m = jnp.maximum(m, jnp.max(parts[0], axis=-1, keepdims=True))
        p = jnp.concatenate([jnp.exp2(x - m) for x in parts], axis=1).astype(BF16)
        o_ref[lo:hi, :] = _pv_normalised(p, v_ref[0:hi, :], first, tq).astype(BF16)


def _fox_prompt(q, k, v, logf_t, *, batch, tq):
    t, d = q.shape
    s_len = t // batch
    tq = min(tq, s_len)
    heads = logf_t.shape[0]
    blk = pl.BlockSpec((s_len, HEAD_PAIR), lambda b, hp: (b, hp))
    kern = functools.partial(_fox_prompt_kernel, tq=tq)
    return pl.pallas_call(
        kern, grid=(batch, d // HEAD_PAIR),
        in_specs=[blk, blk, blk, pl.BlockSpec((1, 2, s_len), lambda b, hp: (hp, 0, b))],
        out_specs=blk, out_shape=jax.ShapeDtypeStruct((t, d), BF16),
        compiler_params=_params(2))(q, k, v, logf_t.reshape(heads // 2, 2, t))


def _fox_sample_kernel(q_ref, k_ref, v_ref, ck_ref, cv_ref, lft_ref, o_ref, *, past):
    n_new = q_ref.shape[0]
    cum = _lane_cumsum(lft_ref[0, 0]) * LOG2E
    first, second = _head_masks((n_new, HEAD_PAIR))
    r = lax.broadcasted_iota(jnp.int32, (n_new, n_new), 0)
    c = lax.broadcasted_iota(jnp.int32, (n_new, n_new), 1)
    causal = c <= r
    q = q_ref[...]
    kc, vc = ck_ref[0], cv_ref[0]
    kn, vn = k_ref[...], v_ref[...]
    outs = []
    for hh, msk in enumerate((first, second)):
        qm = jnp.where(msk, q, jnp.zeros_like(q))
        s_c = _dot_nt(qm, kc) - cum[hh:hh + 1, 0:past]
        s_n = _dot_nt(qm, kn) - cum[hh:hh + 1, past:past + n_new]
        s_n = jnp.where(causal, s_n, NEG_INF)
        outs.append(_softmax_pv([s_c, s_n], [vc, vn]))
    o_ref[...] = jnp.where(first, outs[0], outs[1]).astype(BF16)


def _fox_sample(q, k, v, cache_k, cache_v, logf_all_t, *, batch, past):
    t, d = q.shape
    n_new = t // batch
    padded = logf_all_t.shape[-1]
    row = pl.BlockSpec((n_new, HEAD_PAIR), lambda b, hp: (b, hp))
    cache = pl.BlockSpec((1, past, HEAD_PAIR), lambda b, hp: (b, 0, hp))
    kern = functools.partial(_fox_sample_kernel, past=past)
    return pl.pallas_call(
        kern, grid=(batch, d // HEAD_PAIR),
        in_specs=[row, row, row, cache, cache,
                  pl.BlockSpec((1, 1, 2, padded), lambda b, hp: (b, hp, 0, 0))],
        out_specs=row, out_shape=jax.ShapeDtypeStruct((t, d), BF16),
        compiler_params=_params(2))(q, k, v, cache_k.astype(BF16).reshape(batch, past, d),
                                    cache_v.astype(BF16).reshape(batch, past, d), logf_all_t)


RET_BLOCK = 256


def _ret_proj_kernel(h_ref, w_ref, cos_ref, sin_ref, q_ref, k_ref, v_ref, g_ref, *,
                     heads, key_dim, val_width):
    h = h_ref[...]
    cos, sin = cos_ref[...], sin_ref[...]
    half = key_dim // 2
    qk_width = heads * key_dim
    k_scale = key_dim ** -0.5
    for part, (ref, scale) in enumerate(((q_ref, 1.0), (k_ref, k_scale))):
        for hd in range(heads):
            c0 = part * qk_width + hd * key_dim
            x = _dot(h, w_ref[:, c0:c0 + key_dim])
            x1, x2 = x[:, :half], x[:, half:]
            o0 = hd * key_dim
            ref[:, o0:o0 + half] = ((x1 * cos - x2 * sin) * scale).astype(BF16)
            ref[:, o0 + half:o0 + key_dim] = ((x2 * cos + x1 * sin) * scale).astype(BF16)
    v0 = 2 * qk_width
    step = 4 * MXU_DIM
    for c in range(0, val_width, step):
        v_ref[:, c:c + step] = _dot(h, w_ref[:, v0 + c:v0 + c + step]).astype(BF16)
        g = _dot(h, w_ref[:, v0 + val_width + c:v0 + val_width + c + step])
        g_ref[:, c:c + step] = (g * jax.nn.sigmoid(g)).astype(BF16)


def _ret_proj(h, w, cos, sin, *, heads, tm):
    t, d = h.shape
    tm = min(tm, t, cos.shape[0])
    key_dim = d // heads
    val_width = 2 * d
    n_period = cos.shape[0] // tm
    row = lambda i: (i, 0)
    tab = pl.BlockSpec((tm, key_dim // 2), lambda i: (i % n_period, 0))
    kern = functools.partial(_ret_proj_kernel, heads=heads, key_dim=key_dim, val_width=val_width)
    return pl.pallas_call(
        kern, grid=(t // tm,),
        in_specs=[pl.BlockSpec((tm, d), row), _resident(w.shape), tab, tab],
        out_specs=[pl.BlockSpec((tm, d), row), pl.BlockSpec((tm, d), row),
                   pl.BlockSpec((tm, val_width), row), pl.BlockSpec((tm, val_width), row)],
        out_shape=[jax.ShapeDtypeStruct((t, d), BF16), jax.ShapeDtypeStruct((t, d), BF16),
                   jax.ShapeDtypeStruct((t, val_width), BF16),
                   jax.ShapeDtypeStruct((t, val_width), BF16)],
        compiler_params=_params(1))(h, w.astype(BF16), cos, sin)


def _retention_kernel(*refs, n_blocks, blk, has_init):
    it = iter(refs)
    q_ref, k_ref, v_ref, g_ref, gn_ref = (next(it) for _ in range(5))
    dec_ref, qd_ref, kd_ref, sd_ref = (next(it) for _ in range(4))
    s0_ref = next(it) if has_init else None
    y_ref, so_ref, st_ref = next(it), next(it), next(it)

    if has_init:
        st_ref[...] = s0_ref[0, 0]
    else:
        st_ref[...] = jnp.zeros_like(st_ref)
    decay = dec_ref[0]
    q_decay, k_decay, s_decay = qd_ref[0], kd_ref[0], sd_ref[0]
    for c in range(n_blocks):
        rows = slice(c * blk, (c + 1) * blk)
        q, k, v = q_ref[rows, :], k_ref[rows, :], v_ref[rows, :]
        state = st_ref[...]
        scores = (_dot_nt(q, k) * decay).astype(BF16)
        o = _dot(scores, v) + _dot(q, state.astype(BF16)) * q_decay
        k_dec = (k.astype(F32) * k_decay).astype(BF16)
        st_ref[...] = s_decay * state + _dot_tn(k_dec, v)
        mu = jnp.mean(o, axis=-1, keepdims=True)
        cen = o - mu
        var = jnp.mean(cen * cen, axis=-1, keepdims=True)
        y = g_ref[rows, :].astype(F32) * (cen * lax.rsqrt(var + GN_EPS) * gn_ref[...])
        y_ref[rows, :] = y.astype(BF16)
    so_ref[0, 0] = st_ref[...]


def _retention_decays(heads, blk):
    log_gamma = np.log1p(-np.exp2(-5.0 - np.arange(heads, dtype=np.float64)))
    idx = np.arange(blk, dtype=np.float64)
    diff = idx[:, None] - idx[None, :]
    decay = np.where(diff >= 0, np.exp(np.maximum(diff, 0.0)[None] * log_gamma[:, None, None]), 0.0)
    q_decay = np.exp((idx + 1.0)[None, :] * log_gamma[:, None])[..., None]
    k_decay = np.exp((blk - 1.0 - idx)[None, :] * log_gamma[:, None])[..., None]
    s_decay = np.exp(blk * log_gamma)[:, None, None]
    return tuple(jnp.asarray(a, F32) for a in (decay, q_decay, k_decay, s_decay))


def _retention(q, k, v, gate, gn_g, state0, *, batch, heads):
    t, d = q.shape
    s_len = t // batch
    key_dim = d // heads
    val_dim = v.shape[1] // heads
    blk = min(RET_BLOCK, s_len)
    decay, q_decay, k_decay, s_decay = _retention_decays(heads, blk)
    qk_blk = pl.BlockSpec((s_len, key_dim), lambda b, h: (b, h))
    v_blk = pl.BlockSpec((s_len, val_dim), lambda b, h: (b, h))
    per_head = lambda shape: pl.BlockSpec((1,) + shape, lambda b, h: (h, 0, 0))
    state_blk = pl.BlockSpec((1, 1, key_dim, val_dim), lambda b, h: (b, h, 0, 0))
    in_specs = [qk_blk, qk_blk, v_blk, v_blk, pl.BlockSpec((1, val_dim), lambda b, h: (0, h)),
                per_head((blk, blk)), per_head((blk, 1)), per_head((blk, 1)), per_head((1, 1))]
    args = [q, k, v, gate, gn_g.reshape(1, -1).astype(F32), decay, q_decay, k_decay, s_decay]
    if state0 is not None:
        in_specs.append(state_blk)
        args.append(state0)
    kern = functools.partial(_retention_kernel, n_blocks=s_len // blk, blk=blk,
                             has_init=state0 is not None)
    return pl.pallas_call(
        kern, grid=(batch, heads), in_specs=in_specs, out_specs=[v_blk, state_blk],
        out_shape=[jax.ShapeDtypeStruct(v.shape, BF16),
                   jax.ShapeDtypeStruct((batch, heads, key_dim, val_dim), F32)],
        scratch_shapes=[pltpu.VMEM((key_dim, val_dim), F32)],
        compiler_params=_params(2))(*args)


def _rotary_tables(pos, half):
    inv_freq = ROPE_BASE ** (-jnp.arange(half, dtype=F32) / half)
    ang = pos.astype(F32)[:, None] * inv_freq[None, :]
    return jnp.cos(ang), jnp.sin(ang)


TM_PROMPT = 512


def kernel(x_prompt, x_sample, cache_chunk_k, cache_chunk_v, cache_fox_k, cache_fox_v, cache_fox_logf,
           state_ret, norm_g, w_ffn_in, w_ffn_out, a_w_in, a_g_q, a_g_k, a_rel_table, a_w_out,
           b_w_in, b_bias_f, b_g_q, b_g_k, b_w_out, c_w_in, c_gn_g, c_w_out):
    batch, seq, d = x_prompt.shape
    dec_batch, dec_seq, _ = x_sample.shape
    past = cache_fox_k.shape[2]
    depth = norm_g.shape[0]
    xp = x_prompt.reshape(batch * seq, d)
    xs = x_sample.reshape(dec_batch * dec_seq, d)
    head_dim = d // A_HEADS
    outs = {name: [] for name in ("a_kp", "a_vp", "a_ks", "a_vs", "b_kp", "b_vp", "b_fp",
                                  "b_ks", "b_vs", "b_fs", "c_sp", "c_ss")}

    for i in range(depth):
        g = norm_g[i].astype(F32)
        kind, j = i % N_MIXERS, i // N_MIXERS
        w1_in, w1_out = w_ffn_in[i, 0].astype(BF16), w_ffn_out[i, 0].astype(BF16)
        w2_in, w2_out = w_ffn_in[i, 1].astype(BF16), w_ffn_out[i, 1].astype(BF16)
        gains1 = jnp.stack([g[0], g[0], g[1]])
        gains2 = jnp.stack([g[2], g[3], g[3]])
        xp, hp = _ffn(xp, gains1, w1_in, w1_out, has_post=False, has_hout=True, tm=TM_PROMPT)
        xs, hs = _ffn(xs, gains1, w1_in, w1_out, has_post=False, has_hout=True, tm=TM_PROMPT)

        if kind == 0:
            keep = min(A_WINDOW, seq)
            qp, kbp, vbp, kp, vp = _qkv_proj(hp, a_w_in[j], a_g_q[j], a_g_k[j], heads=A_HEADS,
                                             tail=(seq, keep), tm=TM_PROMPT)
            qs, kbs, vbs, ks, vs = _qkv_proj(hs, a_w_in[j], a_g_q[j], a_g_k[j], heads=A_HEADS,
                                             tm=TM_PROMPT)
            op = _band_prompt(qp, kbp, vbp, a_rel_table[j], batch=batch, tq=256)
            os_ = _band_sample(qs, kbs, vbs, cache_chunk_k[j], cache_chunk_v[j], a_rel_table[j],
                               batch=dec_batch)
            outs["a_kp"].append(kp.reshape(batch, keep, A_HEADS, head_dim))
            outs["a_vp"].append(vp.reshape(batch, keep, A_HEADS, head_dim))
            outs["a_ks"].append(ks.reshape(dec_batch, dec_seq, A_HEADS, head_dim))
            outs["a_vs"].append(vs.reshape(dec_batch, dec_seq, A_HEADS, head_dim))
            w_o = a_w_out[j].astype(BF16)
        elif kind == 1:
            forget = (b_w_in[j][:, 3 * d:], b_bias_f[j])
            qp, kbp, vbp, kp, vp, fp, fpt = _qkv_proj(hp, b_w_in[j], b_g_q[j], b_g_k[j], heads=B_HEADS,
                                                      forget=forget, tm=TM_PROMPT)
            qs, kbs, vbs, ks, vs, fs, _ = _qkv_proj(hs, b_w_in[j], b_g_q[j], b_g_k[j], heads=B_HEADS,
                                                    forget=forget, tm=TM_PROMPT)
            op = _fox_prompt(qp, kbp, vbp, fpt, batch=batch, tq=256)
            total = past + dec_seq
            padded = -(-total // LANES) * LANES
            lf_all = jnp.concatenate([cache_fox_logf[j].astype(F32),
                                      fs.reshape(dec_batch, dec_seq, B_HEADS)], axis=1)
            lf_all = jnp.pad(jnp.swapaxes(lf_all, 1, 2), ((0, 0), (0, 0), (0, padded - total)))
            os_ = _fox_sample(qs, kbs, vbs, cache_fox_k[j], cache_fox_v[j],
                              lf_all.reshape(dec_batch, B_HEADS // 2, 2, padded),
                              batch=dec_batch, past=past)
            outs["b_kp"].append(kp.reshape(batch, seq, B_HEADS, head_dim))
            outs["b_vp"].append(vp.reshape(batch, seq, B_HEADS, head_dim))
            outs["b_fp"].append(fp.reshape(batch, seq, B_HEADS))
            outs["b_ks"].append(ks.reshape(dec_batch, dec_seq, B_HEADS, head_dim))
            outs["b_vs"].append(vs.reshape(dec_batch, dec_seq, B_HEADS, head_dim))
            outs["b_fs"].append(fs.reshape(dec_batch, dec_seq, B_HEADS))
            w_o = b_w_out[j].astype(BF16)
        else:
            half = d // C_HEADS // 2
            cos_p, sin_p = _rotary_tables(jnp.arange(seq), half)
            cos_s, sin_s = _rotary_tables(past + jnp.arange(dec_seq), half)
            cos_s, sin_s = jnp.tile(cos_s, (dec_batch, 1)), jnp.tile(sin_s, (dec_batch, 1))
            qp, kp, vp, gp = _ret_proj(hp, c_w_in[j], cos_p, sin_p, heads=C_HEADS, tm=TM_PROMPT)
            qs, ks, vs, gs = _ret_proj(hs, c_w_in[j], cos_s, sin_s, heads=C_HEADS, tm=TM_PROMPT)
            op, sp = _retention(qp, kp, vp, gp, c_gn_g[j], None, batch=batch, heads=C_HEADS)
            os_, ss = _retention(qs, ks, vs, gs, c_gn_g[j], state_ret[j].astype(F32),
                                 batch=dec_batch, heads=C_HEADS)
            outs["c_sp"].append(sp)
            outs["c_ss"].append(ss)
            w_o = c_w_out[j].astype(BF16)

        xp = _ffn(xp, gains2, w2_in, w2_out, mix=(op, w_o), has_post=True, has_hout=False,
                  tm=TM_PROMPT)
        xs = _ffn(xs, gains2, w2_in, w2_out, mix=(os_, w_o), has_post=True, has_hout=False,
                  tm=TM_PROMPT)

    stack = lambda name: jnp.stack(outs[name])
    return (xp.reshape(batch, seq, d), xs.reshape(dec_batch, dec_seq, d),
            stack("a_kp"), stack("a_vp"), stack("a_ks"), stack("a_vs"),
            stack("b_kp"), stack("b_vp"), stack("b_fp"), stack("b_ks"), stack("b_vs"), stack("b_fs"),
            stack("c_sp"), stack("c_ss"))
```

```python
import functools

import numpy as np
import jax
import jax.numpy as jnp
from jax import lax
from jax.experimental import pallas as pl
from jax.experimental.pallas import tpu as pltpu

CHUNK = 64
A_HEADS = 16
A_LEFT_CHUNKS = 8
A_WINDOW = A_LEFT_CHUNKS * CHUNK
A_BAND = A_WINDOW + CHUNK
A_REL_CLIP = 128
B_HEADS = 16
C_HEADS = 4
ROPE_BASE = 10000.0
NORM_EPS = 1e-6
GN_EPS = 1e-5
NEG_INF = -1e30
N_MIXERS = 3
LOG2E = 1.4426950408889634

LANES = 128
HEAD_PAIR = LANES
MXU_DIM = 256
VMEM_LIMIT = 56 * 1024 * 1024

BF16 = jnp.bfloat16
F32 = jnp.float32


def _params(n_axes, vmem=VMEM_LIMIT):
    return pltpu.CompilerParams(dimension_semantics=("arbitrary",) * n_axes,
                                vmem_limit_bytes=vmem)


def _resident(shape):
    nd = len(shape)
    return pl.BlockSpec(shape, lambda *_: (0,) * nd, pipeline_mode=pl.Buffered(1))


def _dot(a, b):
    return jnp.dot(a, b, preferred_element_type=F32)


def _dot_nt(a, b):
    return lax.dot_general(a, b, (((1,), (1,)), ((), ())), preferred_element_type=F32)


def _dot_tn(a, b):
    return lax.dot_general(a, b, (((0,), (0,)), ((), ())), preferred_element_type=F32)


def _rms(x, g):
    return x * lax.rsqrt(jnp.mean(x * x, axis=-1, keepdims=True) + NORM_EPS) * g


def _ffn_kernel(*refs, has_mix, has_post, has_hout, bounds, d_ff, sub_rows):
    it = iter(refs)
    x_ref = next(it)
    if has_mix:
        o_ref, wo_ref = next(it), next(it)
    g_ref, win_ref, wout_ref = next(it), next(it), next(it)
    xo_ref = next(it)
    ho_ref = next(it) if has_hout else None

    for r in range(0, x_ref.shape[0], sub_rows):
        rows = slice(r, r + sub_rows)
        x = x_ref[rows, :]
        if has_mix:
            x = x + _dot(o_ref[rows, :], wo_ref[...])
        h = _rms(x, g_ref[0:1, :]).astype(BF16)
        y = None
        for a, b in bounds:
            gate = _dot(h, win_ref[:, a:b])
            up = _dot(h, win_ref[:, d_ff + a:d_ff + b])
            act = (gate * jax.nn.sigmoid(gate) * up).astype(BF16)
            part = _dot(act, wout_ref[a:b, :])
            y = part if y is None else y + part
        x = x + 0.5 * y
        if has_post:
            x = _rms(x, g_ref[1:2, :])
        xo_ref[rows, :] = x
        if has_hout:
            ho_ref[rows, :] = _rms(x, g_ref[2:3, :]).astype(BF16)


FFN_SUB_ROWS = MXU_DIM


def _ffn_bounds(d_ff):
    step = 4 * MXU_DIM
    return tuple((a, min(a + step, d_ff)) for a in range(0, d_ff, step))


def _ffn(x, gains, w_in, w_out, *, mix=None, has_post, has_hout, tm):
    t, d = x.shape
    d_ff = w_out.shape[0]
    tm = min(tm, t)
    row = lambda i: (i, 0)
    in_specs = [pl.BlockSpec((tm, d), row)]
    args = [x]
    if mix is not None:
        o, w_o = mix
        in_specs += [pl.BlockSpec((tm, o.shape[1]), row), _resident(w_o.shape)]
        args += [o, w_o]
    in_specs += [_resident(gains.shape), _resident(w_in.shape), _resident(w_out.shape)]
    args += [gains, w_in, w_out]
    out_shape = [jax.ShapeDtypeStruct((t, d), F32)]
    out_specs = [pl.BlockSpec((tm, d), row)]
    if has_hout:
        out_shape.append(jax.ShapeDtypeStruct((t, d), BF16))
        out_specs.append(pl.BlockSpec((tm, d), row))
    kern = functools.partial(_ffn_kernel, has_mix=mix is not None, has_post=has_post,
                             has_hout=has_hout, bounds=_ffn_bounds(d_ff), d_ff=d_ff,
                             sub_rows=min(FFN_SUB_ROWS, tm))
    res = pl.pallas_call(kern, grid=(t // tm,), in_specs=in_specs, out_specs=out_specs,
                         out_shape=out_shape, compiler_params=_params(1))(*args)
    return res if has_hout else res[0]


def _head_mean_square(x, e_ref, head_dim):
    sq = (x * x).astype(BF16)
    parts = [_dot(sq[:, c:c + MXU_DIM], e_ref[...]) for c in range(0, x.shape[1], MXU_DIM)]
    return jnp.concatenate(parts, axis=1) * (1.0 / head_dim)


def _log_sigmoid(x):
    return jnp.minimum(x, 0.0) - jnp.log1p(jnp.exp(-jnp.abs(x)))


PROJ_SUB_ROWS = MXU_DIM


def _qkv_kernel(*refs, d, head_dim, has_f, tail_tiles, sub_rows):
    it = iter(refs)
    h_ref, w_ref, e_ref, gq_ref, gk_ref = (next(it) for _ in range(5))
    if has_f:
        wf_ref, wft_ref, bf_ref, bft_ref = (next(it) for _ in range(4))
    q_ref, kb_ref, vb_ref, kf_ref, vf_ref = (next(it) for _ in range(5))
    if has_f:
        lf_ref, lft_ref = next(it), next(it)

    heads = d // head_dim

    def write_f32(ref, x, r0):
        parts = jnp.stack([x[:, hd * head_dim:(hd + 1) * head_dim] for hd in range(heads)], axis=0)
        ref[r0 * heads:(r0 + x.shape[0]) * heads, :] = (
            pltpu.einshape("htd->thd", parts).reshape(x.shape[0] * heads, head_dim))

    kept_f32 = []
    for r0 in range(0, h_ref.shape[0], sub_rows):
        rows = slice(r0, r0 + sub_rows)
        h = h_ref[rows, :]
        k = _dot(h, w_ref[:, d:2 * d])
        k = k * lax.rsqrt(_head_mean_square(k, e_ref, head_dim) + NORM_EPS) * gk_ref[...]
        kb_ref[rows, :] = k.astype(BF16)
        v = _dot(h, w_ref[:, 2 * d:3 * d])
        vb_ref[rows, :] = v.astype(BF16)
        if tail_tiles is None:
            write_f32(kf_ref, k, r0)
            write_f32(vf_ref, v, r0)
        else:
            kept_f32.append((r0, k, v))
        q = _dot(h, w_ref[:, 0:d])
        q = q * lax.rsqrt(_head_mean_square(q, e_ref, head_dim) + NORM_EPS) * gq_ref[...]
        q_ref[rows, :] = q.astype(BF16)
        if has_f:
            lf_ref[rows, :] = _log_sigmoid(_dot(h, wf_ref[...]) + bf_ref[...])
            lft_ref[:, rows] = _log_sigmoid(_dot_nt(wft_ref[...], h) + bft_ref[...])

    if tail_tiles is not None:
        per_seq, kept = tail_tiles

        @pl.when(pl.program_id(0) % per_seq >= per_seq - kept)
        def _():
            for r0, k, v in kept_f32:
                write_f32(kf_ref, k, r0)
                write_f32(vf_ref, v, r0)


def _qkv_proj(h, w, g_q, g_k, *, heads, forget=None, tail=None, tm):
    t, d = h.shape
    tm = min(tm, t)
    head_dim = d // heads
    scale = head_dim ** -0.5 * LOG2E
    blk = np.kron(np.eye(MXU_DIM // head_dim), np.ones((head_dim, head_dim)))
    e = jnp.asarray(blk, BF16)
    gq = (jnp.tile(g_q.astype(F32), heads) * scale).reshape(1, d)
    gk = jnp.tile(g_k.astype(F32), heads).reshape(1, d)
    row = lambda i: (i, 0)
    if tail is None or tail[0] == tail[1]:
        tail_tiles, f32_rows, f32_row = None, t, row
    else:
        seq, keep = tail
        assert seq % tm == 0 and keep % tm == 0
        per_seq, kept = seq // tm, keep // tm
        tail_tiles, f32_rows = (per_seq, kept), (t // seq) * keep
        f32_row = lambda i: ((i // per_seq) * kept + jnp.maximum(i % per_seq - (per_seq - kept), 0), 0)
    in_specs = [pl.BlockSpec((tm, d), row), _resident((d, 3 * d)), _resident(e.shape),
                _resident((1, d)), _resident((1, d))]
    args = [h, w[:, :3 * d].astype(BF16), e, gq, gk]
    out_shape = ([jax.ShapeDtypeStruct((t, d), BF16)] * 3
                 + [jax.ShapeDtypeStruct((f32_rows * heads, head_dim), F32)] * 2)
    out_specs = ([pl.BlockSpec((tm, d), row)] * 3
                 + [pl.BlockSpec((tm * heads, head_dim), f32_row)] * 2)
    if forget is not None:
        w_f, bias_f = forget
        args += [w_f.astype(BF16), w_f.T.astype(BF16), bias_f.reshape(1, heads).astype(F32),
                 bias_f.reshape(heads, 1).astype(F32)]
        in_specs += [_resident((d, heads)), _resident((heads, d)), _resident((1, heads)),
                     _resident((heads, 1))]
        out_shape += [jax.ShapeDtypeStruct((t, heads), F32), jax.ShapeDtypeStruct((heads, t), F32)]
        out_specs += [pl.BlockSpec((tm, heads), row), pl.BlockSpec((heads, tm), lambda i: (0, i))]
    kern = functools.partial(_qkv_kernel, d=d, head_dim=head_dim, has_f=forget is not None,
                             tail_tiles=tail_tiles, sub_rows=min(PROJ_SUB_ROWS, tm))
    return pl.pallas_call(kern, grid=(t // tm,), in_specs=in_specs, out_specs=out_specs,
                          out_shape=out_shape, compiler_params=_params(1))(*args)


def _head_masks(shape):
    lane = lax.broadcasted_iota(jnp.int32, shape, len(shape) - 1)
    first = lane < (HEAD_PAIR // 2)
    return first, jnp.logical_not(first)


def _softmax_pv(score_parts, value_parts):
    m = None
    for s in score_parts:
        mx = jnp.max(s, axis=-1, keepdims=True)
        m = mx if m is None else jnp.maximum(m, mx)
    acc, l = None, None
    for s, v in zip(score_parts, value_parts):
        p = jnp.exp2(s - m)
        ls = jnp.sum(p, axis=-1, keepdims=True)
        pv = _dot(p.astype(BF16), v)
        acc = pv if acc is None else acc + pv
        l = ls if l is None else l + ls
    return acc * (1.0 / l)


def _stack_heads(q, first, second):
    zero = jnp.zeros_like(q)
    return jnp.concatenate([jnp.where(first, q, zero), jnp.where(second, q, zero)], axis=0)


def _pv_normalised(p, vw, first, tq):
    ones = jnp.ones(vw.shape, vw.dtype)
    o2 = _dot(p, jnp.concatenate([vw, ones], axis=1))
    o2 = o2[:, :HEAD_PAIR] * (1.0 / o2[:, HEAD_PAIR:])
    return jnp.where(first, o2[:tq], o2[tq:])


def _toeplitz_bias(rel_table, n_rows, n_cols, offset):
    n_diag = n_rows + n_cols - 1
    u = np.arange(n_diag) - (n_rows - 1)
    idx = np.clip(offset - u, -A_REL_CLIP, A_REL_CLIP) + A_REL_CLIP
    diag = jnp.pad(rel_table.astype(F32)[:, idx], ((0, 0), (0, 1)))
    heads = diag.shape[0]
    skew = jnp.tile(diag, (1, n_rows))[:, :n_rows * n_diag].reshape(heads, n_rows, n_diag)
    return skew[:, :, n_rows - 1:n_rows - 1 + n_cols]


def _band_prompt_kernel(q_ref, k_ref, v_ref, bm_ref, o_ref, *, tq, window):
    s_len = q_ref.shape[0]
    first, second = _head_masks((tq, HEAD_PAIR))
    width = window + tq
    for t in range(s_len // tq):
        lo, hi = max(0, t * tq - window), (t + 1) * tq
        qs = _stack_heads(q_ref[t * tq:hi, :], first, second)
        s = _dot_nt(qs, k_ref[lo:hi, :]) + bm_ref[0, :, width - (hi - lo):]
        p = jnp.exp2(s - jnp.max(s, axis=-1, keepdims=True)).astype(BF16)
        o_ref[t * tq:hi, :] = _pv_normalised(p, v_ref[lo:hi, :], first, tq).astype(BF16)


def _band_bias_mask(rel_table, tq):
    width = A_WINDOW + tq
    chunk_bias = _toeplitz_bias(rel_table, CHUNK, A_BAND, A_BAND - CHUNK) * LOG2E
    rows = [jnp.pad(chunk_bias, ((0, 0), (0, 0), (c * CHUNK, width - A_BAND - c * CHUNK)),
                    constant_values=NEG_INF) for c in range(tq // CHUNK)]
    bm = jnp.concatenate(rows, axis=1)
    return bm.reshape(bm.shape[0] // 2, 2 * tq, width)


def _band_prompt(q, k, v, rel_table, *, batch, tq):
    t, d = q.shape
    s_len = t // batch
    tq = min(tq, s_len)
    bm = _band_bias_mask(rel_table, tq)
    blk = pl.BlockSpec((s_len, HEAD_PAIR), lambda b, hp: (b, hp))
    kern = functools.partial(_band_prompt_kernel, tq=tq, window=A_WINDOW)
    return pl.pallas_call(
        kern, grid=(batch, d // HEAD_PAIR),
        in_specs=[blk, blk, blk, pl.BlockSpec((1,) + bm.shape[1:], lambda b, hp: (hp, 0, 0))],
        out_specs=blk, out_shape=jax.ShapeDtypeStruct((t, d), BF16),
        compiler_params=_params(2))(q, k, v, bm)


def _rows(parts):
    return jnp.concatenate(parts, axis=0)


def _heads_major(ref, heads):
    rows, head_dim = ref.shape[1:]
    x = ref[0].reshape(rows // heads, heads, head_dim)
    return pltpu.einshape("thd->htd", x).astype(BF16)


def _band_sample_kernel(q_ref, k_ref, v_ref, ck_ref, cv_ref, bias_ref, o_ref, *, heads):
    n_new, d = q_ref.shape
    head_dim = d // heads
    win = ck_ref.shape[1] // heads
    kc, vc = _heads_major(ck_ref, heads), _heads_major(cv_ref, heads)
    cols = [slice(hd * head_dim, (hd + 1) * head_dim) for hd in range(heads)]
    s_c = _rows([_dot_nt(q_ref[:, c], kc[hd]) for hd, c in enumerate(cols)])
    s_n = _rows([_dot_nt(q_ref[:, c], k_ref[:, c]) for c in cols])
    s_c = s_c + bias_ref[:, :win]
    s_n = s_n + bias_ref[:, win:]
    m = jnp.maximum(jnp.max(s_c, axis=-1, keepdims=True), jnp.max(s_n, axis=-1, keepdims=True))
    p_c, p_n = jnp.exp2(s_c - m), jnp.exp2(s_n - m)
    inv_l = 1.0 / (jnp.sum(p_c, axis=-1, keepdims=True) + jnp.sum(p_n, axis=-1, keepdims=True))
    p_c, p_n = p_c.astype(BF16), p_n.astype(BF16)
    for hd, c in enumerate(cols):
        rows = slice(hd * n_new, (hd + 1) * n_new)
        o = _dot(p_c[rows], vc[hd]) + _dot(p_n[rows], v_ref[:, c])
        o_ref[:, c] = (o * inv_l[rows]).astype(BF16)


def _band_sample(q, k, v, cache_k, cache_v, rel_table, *, batch):
    t, d = q.shape
    n_new = t // batch
    _, win, heads, head_dim = cache_k.shape
    bias = (_toeplitz_bias(rel_table, n_new, win + n_new, win) * LOG2E).reshape(heads * n_new, win + n_new)
    row = pl.BlockSpec((n_new, d), lambda b: (b, 0))
    cache = pl.BlockSpec((1, win * heads, head_dim), lambda b: (b, 0, 0))
    kern = functools.partial(_band_sample_kernel, heads=heads)
    return pl.pallas_call(
        kern, grid=(batch,),
        in_specs=[row, row, row, cache, cache, _resident(bias.shape)],
        out_specs=row, out_shape=jax.ShapeDtypeStruct((t, d), BF16),
        compiler_params=_params(1))(q, k, v, cache_k.reshape(batch, win * heads, head_dim),
                                    cache_v.reshape(batch, win * heads, head_dim), bias)


def _lane_cumsum(x):
    n = x.shape[-1]
    lane = lax.broadcasted_iota(jnp.int32, x.shape, x.ndim - 1)
    shift = 1
    while shift < n:
        x = x + jnp.where(lane >= shift, pltpu.roll(x, shift, x.ndim - 1), 0.0)
        shift *= 2
    return x


def _fox_prompt_kernel(q_ref, k_ref, v_ref, lft_ref, o_ref, *, tq):
    s_len = q_ref.shape[0]
    cum = _lane_cumsum(lft_ref[0]) * LOG2E
    first, second = _head_masks((tq, HEAD_PAIR))
    r = lax.broadcasted_iota(jnp.int32, (2 * tq, tq), 0)
    c = lax.broadcasted_iota(jnp.int32, (2 * tq, tq), 1)
    causal = c <= jnp.where(r >= tq, r - tq, r)
    for t in range(s_len // tq):
        lo, hi = t * tq, (t + 1) * tq
        qs = _stack_heads(q_ref[lo:hi, :], first, second)
        s = _dot_nt(qs, k_ref[0:hi, :])
        s = jnp.concatenate([s[:tq] - cum[0:1, 0:hi], s[tq:] - cum[1:2, 0:hi]], axis=0)
        parts = [jnp.where(causal, s[:, lo:hi], NEG_INF)]
        m = jnp.max(parts[0], axis=-1, keepdims=True)
        if t > 0:
            parts.insert(0, s[:, 0:lo])
            m = jnp.maximum(m, jnp.max(parts[0], axis=-1, keepdims=True))
        p = jnp.concatenate([jnp.exp2(x - m) for x in parts], axis=1).astype(BF16)
        o_ref[lo:hi, :] = _pv_normalised(p, v_ref[0:hi, :], first, tq).astype(BF16)


def _fox_prompt(q, k, v, logf_t, *, batch, tq):
    t, d = q.shape
    s_len = t // batch
    tq = min(tq, s_len)
    heads = logf_t.shape[0]
    blk = pl.BlockSpec((s_len, HEAD_PAIR), lambda b, hp: (b, hp))
    kern = functools.partial(_fox_prompt_kernel, tq=tq)
    return pl.pallas_call(
        kern, grid=(batch, d // HEAD_PAIR),
        in_specs=[blk, blk, blk, pl.BlockSpec((1, 2, s_len), lambda b, hp: (hp, 0, b))],
        out_specs=blk, out_shape=jax.ShapeDtypeStruct((t, d), BF16),
        compiler_params=_params(2))(q, k, v, logf_t.reshape(heads // 2, 2, t))


def _logf_cumsum_kernel(lf_ref, cum_ref):
    cum_ref[0] = _lane_cumsum(lf_ref[0]) * LOG2E


def _logf_cumsum(logf_t):
    blk = pl.BlockSpec((1,) + logf_t.shape[1:], lambda b: (b, 0, 0))
    return pl.pallas_call(_logf_cumsum_kernel, grid=(logf_t.shape[0],), in_specs=[blk], out_specs=blk,
                          out_shape=jax.ShapeDtypeStruct(logf_t.shape, F32),
                          compiler_params=_params(1))(logf_t)


FOX_SAMPLE_KEYS = 512


def _fox_sample_kernel(q_ref, k_ref, v_ref, ck_ref, cv_ref, cc_ref, cn_ref, o_ref,
                       m_ref, l_ref, acc_ref, *, heads, tk):
    n_new, d = q_ref.shape
    head_dim = d // heads
    kt = pl.program_id(1)
    cols = [slice(hd * head_dim, (hd + 1) * head_dim) for hd in range(heads)]
    rows = [slice(hd * n_new, (hd + 1) * n_new) for hd in range(heads)]

    @pl.when(kt == 0)
    def _():
        m_ref[...] = jnp.full(m_ref.shape, NEG_INF, F32)
        l_ref[...] = jnp.zeros(l_ref.shape, F32)
        acc_ref[...] = jnp.zeros(acc_ref.shape, F32)

    def update(s, values):
        m_old = m_ref[...]
        m_new = jnp.maximum(m_old, jnp.max(s, axis=-1, keepdims=True))
        alpha = jnp.exp2(m_old - m_new)
        p = jnp.exp2(s - m_new)
        l_ref[...] = alpha * l_ref[...] + jnp.sum(p, axis=-1, keepdims=True)
        p = p.astype(BF16)
        pv = _rows([_dot(p[r], values[hd]) for hd, r in enumerate(rows)])
        acc_ref[...] = alpha * acc_ref[...] + pv
        m_ref[...] = m_new

    kc, vc = _heads_major(ck_ref, heads), _heads_major(cv_ref, heads)
    update(_rows([_dot_nt(q_ref[:, c], kc[hd]) - cc_ref[0, hd:hd + 1, :] for hd, c in enumerate(cols)]),
           [vc[hd] for hd in range(heads)])

    @pl.when(kt == pl.num_programs(1) - 1)
    def _():
        causal = (lax.broadcasted_iota(jnp.int32, (n_new, n_new), 1)
                  <= lax.broadcasted_iota(jnp.int32, (n_new, n_new), 0))
        s = _rows([jnp.where(causal, _dot_nt(q_ref[:, c], k_ref[:, c]) - cn_ref[0, hd:hd + 1, 0:n_new],
                             NEG_INF) for hd, c in enumerate(cols)])
        update(s, [v_ref[:, c] for c in cols])
        o = acc_ref[...] * (1.0 / l_ref[...])
        for hd, c in enumerate(cols):
            o_ref[:, c] = o[rows[hd]].astype(BF16)


def _fox_sample(q, k, v, cache_k, cache_v, cum, *, batch):
    t, d = q.shape
    n_new = t // batch
    _, past, heads, head_dim = cache_k.shape
    tk = min(FOX_SAMPLE_KEYS, past)
    assert past % tk == 0 and past % LANES == 0 and tk % LANES == 0
    row = pl.BlockSpec((n_new, d), lambda b, kt: (b, 0))
    cache = pl.BlockSpec((1, tk * heads, head_dim), lambda b, kt: (b, kt, 0))
    kern = functools.partial(_fox_sample_kernel, heads=heads, tk=tk)
    return pl.pallas_call(
        kern, grid=(batch, past // tk),
        in_specs=[row, row, row, cache, cache,
                  pl.BlockSpec((1, heads, tk), lambda b, kt: (b, 0, kt)),
                  pl.BlockSpec((1, heads, LANES), lambda b, kt: (b, 0, past // LANES))],
        out_specs=row, out_shape=jax.ShapeDtypeStruct((t, d), BF16),
        scratch_shapes=[pltpu.VMEM((heads * n_new, 1), F32), pltpu.VMEM((heads * n_new, 1), F32),
                        pltpu.VMEM((heads * n_new, head_dim), F32)],
        compiler_params=_params(2))(q, k, v, cache_k.reshape(batch, past * heads, head_dim),
                                    cache_v.reshape(batch, past * heads, head_dim), cum, cum)


RET_BLOCK = 256


def _ret_proj_kernel(h_ref, w_ref, cos_ref, sin_ref, q_ref, k_ref, v_ref, g_ref, *,
                     heads, key_dim, val_width):
    h = h_ref[...]
    cos, sin = cos_ref[...], sin_ref[...]
    half = key_dim // 2
    qk_width = heads * key_dim
    k_scale = key_dim ** -0.5
    for part, (ref, scale) in enumerate(((q_ref, 1.0), (k_ref, k_scale))):
        for hd in range(heads):
            c0 = part * qk_width + hd * key_dim
            x = _dot(h, w_ref[:, c0:c0 + key_dim])
            x1, x2 = x[:, :half], x[:, half:]
            o0 = hd * key_dim
            ref[:, o0:o0 + half] = ((x1 * cos - x2 * sin) * scale).astype(BF16)
            ref[:, o0 + half:o0 + key_dim] = ((x2 * cos + x1 * sin) * scale).astype(BF16)
    v0 = 2 * qk_width
    step = 4 * MXU_DIM
    for c in range(0, val_width, step):
        v_ref[:, c:c + step] = _dot(h, w_ref[:, v0 + c:v0 + c + step]).astype(BF16)
        g = _dot(h, w_ref[:, v0 + val_width + c:v0 + val_width + c + step])
        g_ref[:, c:c + step] = (g * jax.nn.sigmoid(g)).astype(BF16)


def _ret_proj(h, w, cos, sin, *, heads, tm):
    t, d = h.shape
    tm = min(tm, t, cos.shape[0])
    key_dim = d // heads
    val_width = 2 * d
    n_period = cos.shape[0] // tm
    row = lambda i: (i, 0)
    tab = pl.BlockSpec((tm, key_dim // 2), lambda i: (i % n_period, 0))
    kern = functools.partial(_ret_proj_kernel, heads=heads, key_dim=key_dim, val_width=val_width)
    return pl.pallas_call(
        kern, grid=(t // tm,),
        in_specs=[pl.BlockSpec((tm, d), row), _resident(w.shape), tab, tab],
        out_specs=[pl.BlockSpec((tm, d), row), pl.BlockSpec((tm, d), row),
                   pl.BlockSpec((tm, val_width), row), pl.BlockSpec((tm, val_width), row)],
        out_shape=[jax.ShapeDtypeStruct((t, d), BF16), jax.ShapeDtypeStruct((t, d), BF16),
                   jax.ShapeDtypeStruct((t, val_width), BF16),
                   jax.ShapeDtypeStruct((t, val_width), BF16)],
        compiler_params=_params(1))(h, w.astype(BF16), cos, sin)


def _retention_kernel(*refs, n_blocks, blk, has_init):
    it = iter(refs)
    q_ref, k_ref, v_ref, g_ref, gn_ref = (next(it) for _ in range(5))
    dec_ref, qd_ref, kd_ref, sd_ref = (next(it) for _ in range(4))
    s0_ref = next(it) if has_init else None
    y_ref, so_ref, st_ref = next(it), next(it), next(it)

    if has_init:
        st_ref[...] = s0_ref[0, 0]
    else:
        st_ref[...] = jnp.zeros_like(st_ref)
    decay = dec_ref[0]
    q_decay, k_decay, s_decay = qd_ref[0], kd_ref[0], sd_ref[0]
    for c in range(n_blocks):
        rows = slice(c * blk, (c + 1) * blk)
        q, k, v = q_ref[rows, :], k_ref[rows, :], v_ref[rows, :]
        state = st_ref[...]
        scores = (_dot_nt(q, k) * decay).astype(BF16)
        o = _dot(scores, v) + _dot(q, state.astype(BF16)) * q_decay
        k_dec = (k.astype(F32) * k_decay).astype(BF16)
        st_ref[...] = s_decay * state + _dot_tn(k_dec, v)
        mu = jnp.mean(o, axis=-1, keepdims=True)
        cen = o - mu
        var = jnp.mean(cen * cen, axis=-1, keepdims=True)
        y = g_ref[rows, :].astype(F32) * (cen * lax.rsqrt(var + GN_EPS) * gn_ref[...])
        y_ref[rows, :] = y.astype(BF16)
    so_ref[0, 0] = st_ref[...]


def _retention_decays(heads, blk):
    log_gamma = np.log1p(-np.exp2(-5.0 - np.arange(heads, dtype=np.float64)))
    idx = np.arange(blk, dtype=np.float64)
    diff = idx[:, None] - idx[None, :]
    decay = np.where(diff >= 0, np.exp(np.maximum(diff, 0.0)[None] * log_gamma[:, None, None]), 0.0)
    q_decay = np.exp((idx + 1.0)[None, :] * log_gamma[:, None])[..., None]
    k_decay = np.exp((blk - 1.0 - idx)[None, :] * log_gamma[:, None])[..., None]
    s_decay = np.exp(blk * log_gamma)[:, None, None]
    return tuple(jnp.asarray(a, F32) for a in (decay, q_decay, k_decay, s_decay))


def _retention(q, k, v, gate, gn_g, state0, *, batch, heads):
    t, d = q.shape
    s_len = t // batch
    key_dim = d // heads
    val_dim = v.shape[1] // heads
    blk = min(RET_BLOCK, s_len)
    decay, q_decay, k_decay, s_decay = _retention_decays(heads, blk)
    qk_blk = pl.BlockSpec((s_len, key_dim), lambda b, h: (b, h))
    v_blk = pl.BlockSpec((s_len, val_dim), lambda b, h: (b, h))
    per_head = lambda shape: pl.BlockSpec((1,) + shape, lambda b, h: (h, 0, 0))
    state_blk = pl.BlockSpec((1, 1, key_dim, val_dim), lambda b, h: (b, h, 0, 0))
    in_specs = [qk_blk, qk_blk, v_blk, v_blk, pl.BlockSpec((1, val_dim), lambda b, h: (0, h)),
                per_head((blk, blk)), per_head((blk, 1)), per_head((blk, 1)), per_head((1, 1))]
    args = [q, k, v, gate, gn_g.reshape(1, -1).astype(F32), decay, q_decay, k_decay, s_decay]
    if state0 is not None:
        in_specs.append(state_blk)
        args.append(state0)
    kern = functools.partial(_retention_kernel, n_blocks=s_len // blk, blk=blk,
                             has_init=state0 is not None)
    return pl.pallas_call(
        kern, grid=(batch, heads), in_specs=in_specs, out_specs=[v_blk, state_blk],
        out_shape=[jax.ShapeDtypeStruct(v.shape, BF16),
                   jax.ShapeDtypeStruct((batch, heads, key_dim, val_dim), F32)],
        scratch_shapes=[pltpu.VMEM((key_dim, val_dim), F32)],
        compiler_params=_params(2))(*args)


def _rotary_tables(pos, half):
    inv_freq = ROPE_BASE ** (-jnp.arange(half, dtype=F32) / half)
    ang = pos.astype(F32)[:, None] * inv_freq[None, :]
    return jnp.cos(ang), jnp.sin(ang)


TM_PROMPT = 512
TM_FFN = 1024


def kernel(x_prompt, x_sample, cache_chunk_k, cache_chunk_v, cache_fox_k, cache_fox_v, cache_fox_logf,
           state_ret, norm_g, w_ffn_in, w_ffn_out, a_w_in, a_g_q, a_g_k, a_rel_table, a_w_out,
           b_w_in, b_bias_f, b_g_q, b_g_k, b_w_out, c_w_in, c_gn_g, c_w_out):
    batch, seq, d = x_prompt.shape
    dec_batch, dec_seq, _ = x_sample.shape
    past = cache_fox_k.shape[2]
    depth = norm_g.shape[0]
    xp = x_prompt.reshape(batch * seq, d)
    xs = x_sample.reshape(dec_batch * dec_seq, d)
    head_dim = d // A_HEADS
    outs = {name: [] for name in ("a_kp", "a_vp", "a_ks", "a_vs", "b_kp", "b_vp", "b_fp",
                                  "b_ks", "b_vs", "b_fs", "c_sp", "c_ss")}

    for i in range(depth):
        g = norm_g[i].astype(F32)
        kind, j = i % N_MIXERS, i // N_MIXERS
        w1_in, w1_out = w_ffn_in[i, 0].astype(BF16), w_ffn_out[i, 0].astype(BF16)
        w2_in, w2_out = w_ffn_in[i, 1].astype(BF16), w_ffn_out[i, 1].astype(BF16)
        gains1 = jnp.stack([g[0], g[0], g[1]])
        gains2 = jnp.stack([g[2], g[3], g[3]])
        xp, hp = _ffn(xp, gains1, w1_in, w1_out, has_post=False, has_hout=True, tm=TM_FFN)
        xs, hs = _ffn(xs, gains1, w1_in, w1_out, has_post=False, has_hout=True, tm=TM_FFN)

        if kind == 0:
            keep = min(A_WINDOW, seq)
            qp, kbp, vbp, kp, vp = _qkv_proj(hp, a_w_in[j], a_g_q[j], a_g_k[j], heads=A_HEADS,
                                             tail=(seq, keep), tm=TM_PROMPT)
            qs, kbs, vbs, ks, vs = _qkv_proj(hs, a_w_in[j], a_g_q[j], a_g_k[j], heads=A_HEADS,
                                             tm=TM_PROMPT)
            op = _band_prompt(qp, kbp, vbp, a_rel_table[j], batch=batch, tq=256)
            os_ = _band_sample(qs, kbs, vbs, cache_chunk_k[j], cache_chunk_v[j], a_rel_table[j],
                               batch=dec_batch)
            outs["a_kp"].append(kp.reshape(batch, keep, A_HEADS, head_dim))
            outs["a_vp"].append(vp.reshape(batch, keep, A_HEADS, head_dim))
            outs["a_ks"].append(ks.reshape(dec_batch, dec_seq, A_HEADS, head_dim))
            outs["a_vs"].append(vs.reshape(dec_batch, dec_seq, A_HEADS, head_dim))
            w_o = a_w_out[j].astype(BF16)
        elif kind == 1:
            forget = (b_w_in[j][:, 3 * d:], b_bias_f[j])
            qp, kbp, vbp, kp, vp, fp, fpt = _qkv_proj(hp, b_w_in[j], b_g_q[j], b_g_k[j], heads=B_HEADS,
                                                      forget=forget, tm=TM_PROMPT)
            qs, kbs, vbs, ks, vs, fs, _ = _qkv_proj(hs, b_w_in[j], b_g_q[j], b_g_k[j], heads=B_HEADS,
                                                    forget=forget, tm=TM_PROMPT)
            op = _fox_prompt(qp, kbp, vbp, fpt, batch=batch, tq=256)
            total = past + dec_seq
            padded = -(-total // LANES) * LANES
            lf_all = jnp.concatenate([cache_fox_logf[j].astype(F32),
                                      fs.reshape(dec_batch, dec_seq, B_HEADS)], axis=1)
            lf_all = jnp.pad(jnp.swapaxes(lf_all, 1, 2), ((0, 0), (0, 0), (0, padded - total)))
            os_ = _fox_sample(qs, kbs, vbs, cache_fox_k[j], cache_fox_v[j], _logf_cumsum(lf_all),
                              batch=dec_batch)
            outs["b_kp"].append(kp.reshape(batch, seq, B_HEADS, head_dim))
            outs["b_vp"].append(vp.reshape(batch, seq, B_HEADS, head_dim))
            outs["b_fp"].append(fp.reshape(batch, seq, B_HEADS))
            outs["b_ks"].append(ks.reshape(dec_batch, dec_seq, B_HEADS, head_dim))
            outs["b_vs"].append(vs.reshape(dec_batch, dec_seq, B_HEADS, head_dim))
            outs["b_fs"].append(fs.reshape(dec_batch, dec_seq, B_HEADS))
            w_o = b_w_out[j].astype(BF16)
        else:
            half = d // C_HEADS // 2
            cos_p, sin_p = _rotary_tables(jnp.arange(seq), half)
            cos_s, sin_s = _rotary_tables(past + jnp.arange(dec_seq), half)
            cos_s, sin_s = jnp.tile(cos_s, (dec_batch, 1)), jnp.tile(sin_s, (dec_batch, 1))
            qp, kp, vp, gp = _ret_proj(hp, c_w_in[j], cos_p, sin_p, heads=C_HEADS, tm=TM_PROMPT)
            qs, ks, vs, gs = _ret_proj(hs, c_w_in[j], cos_s, sin_s, heads=C_HEADS, tm=TM_PROMPT)
            op, sp = _retention(qp, kp, vp, gp, c_gn_g[j], None, batch=batch, heads=C_HEADS)
            os_, ss = _retention(qs, ks, vs, gs, c_gn_g[j], state_ret[j].astype(F32),
                                 batch=dec_batch, heads=C_HEADS)
            outs["c_sp"].append(sp)
            outs["c_ss"].append(ss)
            w_o = c_w_out[j].astype(BF16)

        xp = _ffn(xp, gains2, w2_in, w2_out, mix=(op, w_o), has_post=True, has_hout=False,
                  tm=TM_FFN)
        xs = _ffn(xs, gains2, w2_in, w2_out, mix=(os_, w_o), has_post=True, has_hout=False,
                  tm=TM_FFN)

    stack = lambda name: jnp.stack(outs[name])
    return (xp.reshape(batch, seq, d), xs.reshape(dec_batch, dec_seq, d),
            stack("a_kp"), stack("a_vp"), stack("a_ks"), stack("a_vs"),
            stack("b_kp"), stack("b_vp"), stack("b_fp"), stack("b_ks"), stack("b_vs"), stack("b_fs"),
            stack("c_sp"), stack("c_ss"))
```

```python
import functools

import numpy as np
import jax
import jax.numpy as jnp
from jax import lax
from jax.experimental import pallas as pl
from jax.experimental.pallas import tpu as pltpu

CHUNK = 64
A_HEADS = 16
A_LEFT_CHUNKS = 8
A_WINDOW = A_LEFT_CHUNKS * CHUNK
A_BAND = A_WINDOW + CHUNK
A_REL_CLIP = 128
B_HEADS = 16
C_HEADS = 4
ROPE_BASE = 10000.0
NORM_EPS = 1e-6
GN_EPS = 1e-5
NEG_INF = -1e30
N_MIXERS = 3
LOG2E = 1.4426950408889634

LANES = 128
HEAD_PAIR = LANES
MXU_DIM = 256
VMEM_LIMIT = 56 * 1024 * 1024

BF16 = jnp.bfloat16
F32 = jnp.float32


def _params(n_axes, vmem=VMEM_LIMIT):
    return pltpu.CompilerParams(dimension_semantics=("arbitrary",) * n_axes,
                                vmem_limit_bytes=vmem)


def _resident(shape):
    nd = len(shape)
    return pl.BlockSpec(shape, lambda *_: (0,) * nd, pipeline_mode=pl.Buffered(1))


def _dot(a, b):
    return jnp.dot(a, b, preferred_element_type=F32)


def _dot_nt(a, b):
    return lax.dot_general(a, b, (((1,), (1,)), ((), ())), preferred_element_type=F32)


def _dot_tn(a, b):
    return lax.dot_general(a, b, (((0,), (0,)), ((), ())), preferred_element_type=F32)


def _rms(x, g):
    return x * lax.rsqrt(jnp.mean(x * x, axis=-1, keepdims=True) + NORM_EPS) * g


def _ffn_kernel(*refs, has_mix, has_post, has_hout, bounds, d_ff, sub_rows):
    it = iter(refs)
    x_ref = next(it)
    if has_mix:
        o_ref, wo_ref = next(it), next(it)
    g_ref, win_ref, wout_ref = next(it), next(it), next(it)
    xo_ref = next(it)
    ho_ref = next(it) if has_hout else None

    for r in range(0, x_ref.shape[0], sub_rows):
        rows = slice(r, r + sub_rows)
        x = x_ref[rows, :]
        if has_mix:
            x = x + _dot(o_ref[rows, :], wo_ref[...])
        h = _rms(x, g_ref[0:1, :]).astype(BF16)
        y = None
        for a, b in bounds:
            gate = _dot(h, win_ref[:, a:b])
            up = _dot(h, win_ref[:, d_ff + a:d_ff + b])
            act = (gate * jax.nn.sigmoid(gate) * up).astype(BF16)
            part = _dot(act, wout_ref[a:b, :])
            y = part if y is None else y + part
        x = x + 0.5 * y
        if has_post:
            x = _rms(x, g_ref[1:2, :])
        xo_ref[rows, :] = x
        if has_hout:
            ho_ref[rows, :] = _rms(x, g_ref[2:3, :]).astype(BF16)


FFN_SUB_ROWS = MXU_DIM


def _ffn_bounds(d_ff):
    step = 4 * MXU_DIM
    return tuple((a, min(a + step, d_ff)) for a in range(0, d_ff, step))


def _ffn(x, gains, w_in, w_out, *, mix=None, has_post, has_hout, tm):
    t, d = x.shape
    d_ff = w_out.shape[0]
    tm = min(tm, t)
    row = lambda i: (i, 0)
    in_specs = [pl.BlockSpec((tm, d), row)]
    args = [x]
    if mix is not None:
        o, w_o = mix
        in_specs += [pl.BlockSpec((tm, o.shape[1]), row), _resident(w_o.shape)]
        args += [o, w_o]
    in_specs += [_resident(gains.shape), _resident(w_in.shape), _resident(w_out.shape)]
    args += [gains, w_in, w_out]
    out_shape = [jax.ShapeDtypeStruct((t, d), F32)]
    out_specs = [pl.BlockSpec((tm, d), row)]
    if has_hout:
        out_shape.append(jax.ShapeDtypeStruct((t, d), BF16))
        out_specs.append(pl.BlockSpec((tm, d), row))
    kern = functools.partial(_ffn_kernel, has_mix=mix is not None, has_post=has_post,
                             has_hout=has_hout, bounds=_ffn_bounds(d_ff), d_ff=d_ff,
                             sub_rows=min(FFN_SUB_ROWS, tm))
    res = pl.pallas_call(kern, grid=(t // tm,), in_specs=in_specs, out_specs=out_specs,
                         out_shape=out_shape, compiler_params=_params(1))(*args)
    return res if has_hout else res[0]


def _head_mean_square(x, e_ref, head_dim):
    sq = (x * x).astype(BF16)
    parts = [_dot(sq[:, c:c + MXU_DIM], e_ref[...]) for c in range(0, x.shape[1], MXU_DIM)]
    return jnp.concatenate(parts, axis=1) * (1.0 / head_dim)


def _log_sigmoid(x):
    return jnp.minimum(x, 0.0) - jnp.log1p(jnp.exp(-jnp.abs(x)))


PROJ_SUB_ROWS = MXU_DIM


def _qkv_kernel(*refs, d, head_dim, has_f, feature_major, tail_tiles, sub_rows):
    it = iter(refs)
    h_ref, wq_ref, wk_ref, wv_ref, e_ref, gq_ref, gk_ref = (next(it) for _ in range(7))
    if has_f:
        wf_ref, bf_ref = next(it), next(it)
    q_ref, kb_ref, vb_ref, kf_ref, vf_ref = (next(it) for _ in range(5))
    lf_ref = next(it) if has_f else None
    heads = d // head_dim

    kept_f32 = []
    for r0 in range(0, h_ref.shape[0], sub_rows):
        rows = slice(r0, r0 + sub_rows)
        h = h_ref[rows, :]
        if feature_major:
            k = _dot_nt(wk_ref[...], h).reshape(heads, head_dim, sub_rows)
            ms = jnp.mean(k * k, axis=1, keepdims=True)
            k = (k * lax.rsqrt(ms + NORM_EPS)).reshape(d, sub_rows) * gk_ref[...]
            v = _dot_nt(wv_ref[...], h)
            out = (0, slice(None), rows)
        else:
            k = _dot(h, wk_ref[...])
            k = k * lax.rsqrt(_head_mean_square(k, e_ref, head_dim) + NORM_EPS) * gk_ref[...]
            v = _dot(h, wv_ref[...])
            out = (rows, slice(None))
        kb_ref[out] = k.astype(BF16)
        vb_ref[out] = v.astype(BF16)
        if tail_tiles is None:
            kf_ref[out] = k
            vf_ref[out] = v
        else:
            kept_f32.append((out, k, v))
        q = _dot(h, wq_ref[...])
        q = q * lax.rsqrt(_head_mean_square(q, e_ref, head_dim) + NORM_EPS) * gq_ref[...]
        q_ref[rows, :] = q.astype(BF16)
        if has_f:
            if feature_major:
                lf_ref[out] = _log_sigmoid(_dot_nt(wf_ref[...], h) + bf_ref[...])
            else:
                lf_ref[out] = _log_sigmoid(_dot(h, wf_ref[...]) + bf_ref[...])

    if tail_tiles is not None:
        per_seq, kept = tail_tiles

        @pl.when(pl.program_id(0) % per_seq >= per_seq - kept)
        def _():
            for out, k, v in kept_f32:
                kf_ref[out] = k
                vf_ref[out] = v


def _qkv_proj(h, w, g_q, g_k, *, heads, seq=None, keep=None, forget=None, tm):
    t, d = h.shape
    tm = min(tm, t)
    head_dim = d // heads
    scale = head_dim ** -0.5 * LOG2E
    blk = np.kron(np.eye(MXU_DIM // head_dim), np.ones((head_dim, head_dim)))
    e = jnp.asarray(blk, BF16)
    gq = (jnp.tile(g_q.astype(F32), heads) * scale).reshape(1, d)
    gk = jnp.tile(g_k.astype(F32), heads)
    wq, wk, wv = (w[:, c * d:(c + 1) * d].astype(BF16) for c in range(3))
    row = lambda i: (i, 0)
    tail_tiles = None
    if seq is None:
        gk = gk.reshape(1, d)
        kv_shape, kv_spec = (t, d), pl.BlockSpec((tm, d), row)
        f32_shape, f32_spec = kv_shape, kv_spec
    else:
        keep = seq if keep is None else keep
        assert seq % tm == 0 and keep % tm == 0
        per_seq, kept = seq // tm, keep // tm
        gk, wk, wv = gk.reshape(d, 1), wk.T, wv.T
        kv_shape = (t // seq, d, seq)
        kv_spec = pl.BlockSpec((1, d, tm), lambda i: (i // per_seq, 0, i % per_seq))
        f32_shape, f32_spec = kv_shape, kv_spec
        if kept != per_seq:
            tail_tiles = (per_seq, kept)
            f32_shape = (t // seq, d, keep)
            f32_spec = pl.BlockSpec((1, d, tm), lambda i: (i // per_seq, 0,
                                                           jnp.maximum(i % per_seq - (per_seq - kept), 0)))
    in_specs = [pl.BlockSpec((tm, d), row)] + [_resident((d, d))] * 3 + [
        _resident(e.shape), _resident(gq.shape), _resident(gk.shape)]
    args = [h, wq, wk, wv, e, gq, gk]
    out_shape = ([jax.ShapeDtypeStruct((t, d), BF16)] + [jax.ShapeDtypeStruct(kv_shape, BF16)] * 2
                 + [jax.ShapeDtypeStruct(f32_shape, F32)] * 2)
    out_specs = [pl.BlockSpec((tm, d), row), kv_spec, kv_spec, f32_spec, f32_spec]
    if forget is not None:
        w_f, bias_f = forget
        if seq is None:
            w_f, bias_f = w_f.astype(BF16), bias_f.reshape(1, heads).astype(F32)
            out_shape.append(jax.ShapeDtypeStruct((t, heads), F32))
            out_specs.append(pl.BlockSpec((tm, heads), row))
        else:
            w_f, bias_f = w_f.T.astype(BF16), bias_f.reshape(heads, 1).astype(F32)
            out_shape.append(jax.ShapeDtypeStruct((t // seq, heads, seq), F32))
            out_specs.append(pl.BlockSpec((1, heads, tm), kv_spec.index_map))
        args += [w_f, bias_f]
        in_specs += [_resident(w_f.shape), _resident(bias_f.shape)]
    kern = functools.partial(_qkv_kernel, d=d, head_dim=head_dim, has_f=forget is not None,
                             feature_major=seq is not None, tail_tiles=tail_tiles,
                             sub_rows=min(PROJ_SUB_ROWS, tm))
    return pl.pallas_call(kern, grid=(t // tm,), in_specs=in_specs, out_specs=out_specs,
                          out_shape=out_shape, compiler_params=_params(1))(*args)


def _head_masks(shape):
    lane = lax.broadcasted_iota(jnp.int32, shape, len(shape) - 1)
    first = lane < (HEAD_PAIR // 2)
    return first, jnp.logical_not(first)


def _softmax_pv(score_parts, value_parts):
    m = None
    for s in score_parts:
        mx = jnp.max(s, axis=-1, keepdims=True)
        m = mx if m is None else jnp.maximum(m, mx)
    acc, l = None, None
    for s, v in zip(score_parts, value_parts):
        p = jnp.exp2(s - m)
        ls = jnp.sum(p, axis=-1, keepdims=True)
        pv = _dot(p.astype(BF16), v)
        acc = pv if acc is None else acc + pv
        l = ls if l is None else l + ls
    return acc * (1.0 / l)


def _stack_heads(q, first, second):
    zero = jnp.zeros_like(q)
    return jnp.concatenate([jnp.where(first, q, zero), jnp.where(second, q, zero)], axis=0)


def _pv_normalised(p, vt, first, tq):
    ones = jnp.ones(vt.shape, vt.dtype)
    o2 = _dot_nt(p, jnp.concatenate([vt, ones], axis=0))
    o2 = o2[:, :HEAD_PAIR] * (1.0 / o2[:, HEAD_PAIR:])
    return jnp.where(first, o2[:tq], o2[tq:])


def _toeplitz_bias(rel_table, n_rows, n_cols, offset):
    n_diag = n_rows + n_cols - 1
    u = np.arange(n_diag) - (n_rows - 1)
    idx = np.clip(offset - u, -A_REL_CLIP, A_REL_CLIP) + A_REL_CLIP
    diag = jnp.pad(rel_table.astype(F32)[:, idx], ((0, 0), (0, 1)))
    heads = diag.shape[0]
    skew = jnp.tile(diag, (1, n_rows))[:, :n_rows * n_diag].reshape(heads, n_rows, n_diag)
    return skew[:, :, n_rows - 1:n_rows - 1 + n_cols]


def _band_prompt_kernel(q_ref, k_ref, v_ref, bm_ref, o_ref, *, tq, window):
    s_len = q_ref.shape[0]
    first, second = _head_masks((tq, HEAD_PAIR))
    width = window + tq
    for t in range(s_len // tq):
        lo, hi = max(0, t * tq - window), (t + 1) * tq
        qs = _stack_heads(q_ref[t * tq:hi, :], first, second)
        s = _dot(qs, k_ref[0, :, lo:hi]) + bm_ref[0, :, width - (hi - lo):]
        p = jnp.exp2(s - jnp.max(s, axis=-1, keepdims=True)).astype(BF16)
        o_ref[t * tq:hi, :] = _pv_normalised(p, v_ref[0, :, lo:hi], first, tq).astype(BF16)


def _band_bias_mask(rel_table, tq):
    width = A_WINDOW + tq
    chunk_bias = _toeplitz_bias(rel_table, CHUNK, A_BAND, A_BAND - CHUNK) * LOG2E
    rows = [jnp.pad(chunk_bias, ((0, 0), (0, 0), (c * CHUNK, width - A_BAND - c * CHUNK)),
                    constant_values=NEG_INF) for c in range(tq // CHUNK)]
    bm = jnp.concatenate(rows, axis=1)
    return bm.reshape(bm.shape[0] // 2, 2 * tq, width)


def _band_prompt(q, k, v, rel_table, *, batch, tq):
    t, d = q.shape
    s_len = t // batch
    tq = min(tq, s_len)
    bm = _band_bias_mask(rel_table, tq)
    blk = pl.BlockSpec((s_len, HEAD_PAIR), lambda b, hp: (b, hp))
    kv_blk = pl.BlockSpec((1, HEAD_PAIR, s_len), lambda b, hp: (b, hp, 0))
    kern = functools.partial(_band_prompt_kernel, tq=tq, window=A_WINDOW)
    return pl.pallas_call(
        kern, grid=(batch, d // HEAD_PAIR),
        in_specs=[blk, kv_blk, kv_blk, pl.BlockSpec((1,) + bm.shape[1:], lambda b, hp: (hp, 0, 0))],
        out_specs=blk, out_shape=jax.ShapeDtypeStruct((t, d), BF16),
        compiler_params=_params(2))(q, k, v, bm)


def _rows(parts):
    return jnp.concatenate(parts, axis=0)


def _feature_major_cache(cache):
    b, n, heads, head_dim = cache.shape
    return jnp.transpose(cache, (0, 2, 3, 1)).reshape(b, heads * head_dim, n)


def _token_major_heads(x_t, heads):
    b, d, n = x_t.shape
    return jnp.transpose(x_t.reshape(b, heads, d // heads, n), (0, 3, 1, 2))


def _band_sample_kernel(q_ref, k_ref, v_ref, ck_ref, cv_ref, bias_ref, o_ref, *, heads):
    n_new, d = q_ref.shape
    head_dim = d // heads
    win = ck_ref.shape[2]
    cols = [slice(hd * head_dim, (hd + 1) * head_dim) for hd in range(heads)]
    s_c = _rows([_dot(q_ref[:, c], ck_ref[0, c, :].astype(BF16)) for c in cols])
    s_n = _rows([_dot_nt(q_ref[:, c], k_ref[:, c]) for c in cols])
    s_c = s_c + bias_ref[:, :win]
    s_n = s_n + bias_ref[:, win:]
    m = jnp.maximum(jnp.max(s_c, axis=-1, keepdims=True), jnp.max(s_n, axis=-1, keepdims=True))
    p_c, p_n = jnp.exp2(s_c - m), jnp.exp2(s_n - m)
    inv_l = 1.0 / (jnp.sum(p_c, axis=-1, keepdims=True) + jnp.sum(p_n, axis=-1, keepdims=True))
    p_c, p_n = p_c.astype(BF16), p_n.astype(BF16)
    for hd, c in enumerate(cols):
        rows = slice(hd * n_new, (hd + 1) * n_new)
        o = _dot_nt(p_c[rows], cv_ref[0, c, :].astype(BF16)) + _dot(p_n[rows], v_ref[:, c])
        o_ref[:, c] = (o * inv_l[rows]).astype(BF16)


def _band_sample(q, k, v, cache_k, cache_v, rel_table, *, batch):
    t, d = q.shape
    n_new = t // batch
    _, win, heads, _ = cache_k.shape
    bias = (_toeplitz_bias(rel_table, n_new, win + n_new, win) * LOG2E).reshape(heads * n_new, win + n_new)
    row = pl.BlockSpec((n_new, d), lambda b: (b, 0))
    cache = pl.BlockSpec((1, d, win), lambda b: (b, 0, 0))
    kern = functools.partial(_band_sample_kernel, heads=heads)
    return pl.pallas_call(
        kern, grid=(batch,),
        in_specs=[row, row, row, cache, cache, _resident(bias.shape)],
        out_specs=row, out_shape=jax.ShapeDtypeStruct((t, d), BF16),
        compiler_params=_params(1))(q, k, v, _feature_major_cache(cache_k),
                                    _feature_major_cache(cache_v), bias)


def _lane_cumsum(x):
    n = x.shape[-1]
    lane = lax.broadcasted_iota(jnp.int32, x.shape, x.ndim - 1)
    shift = 1
    while shift < n:
        x = x + jnp.where(lane >= shift, pltpu.roll(x, shift, x.ndim - 1), 0.0)
        shift *= 2
    return x


def _fox_prompt_kernel(q_ref, k_ref, v_ref, lft_ref, o_ref, *, tq):
    s_len = q_ref.shape[0]
    cum = _lane_cumsum(lft_ref[0, 0]) * LOG2E
    first, second = _head_masks((tq, HEAD_PAIR))
    r = lax.broadcasted_iota(jnp.int32, (2 * tq, tq), 0)
    c = lax.broadcasted_iota(jnp.int32, (2 * tq, tq), 1)
    causal = c <= jnp.where(r >= tq, r - tq, r)
    for t in range(s_len // tq):
        lo, hi = t * tq, (t + 1) * tq
        qs = _stack_heads(q_ref[lo:hi, :], first, second)
        s = _dot(qs, k_ref[0, :, 0:hi])
        s = jnp.concatenate([s[:tq] - cum[0:1, 0:hi], s[tq:] - cum[1:2, 0:hi]], axis=0)
        parts = [jnp.where(causal, s[:, lo:hi], NEG_INF)]
        m = jnp.max(parts[0], axis=-1, keepdims=True)
        if t > 0:
            parts.insert(0, s[:, 0:lo])
            m = jnp.maximum(m, jnp.max(parts[0], axis=-1, keepdims=True))
        p = jnp.concatenate([jnp.exp2(x - m) for x in parts], axis=1).astype(BF16)
        o_ref[lo:hi, :] = _pv_normalised(p, v_ref[0, :, 0:hi], first, tq).astype(BF16)


def _fox_prompt(q, k, v, logf_t, *, batch, tq):
    t, d = q.shape
    s_len = t // batch
    tq = min(tq, s_len)
    heads = logf_t.shape[1]
    blk = pl.BlockSpec((s_len, HEAD_PAIR), lambda b, hp: (b, hp))
    kv_blk = pl.BlockSpec((1, HEAD_PAIR, s_len), lambda b, hp: (b, hp, 0))
    kern = functools.partial(_fox_prompt_kernel, tq=tq)
    return pl.pallas_call(
        kern, grid=(batch, d // HEAD_PAIR),
        in_specs=[blk, kv_blk, kv_blk, pl.BlockSpec((1, 1, 2, s_len), lambda b, hp: (b, hp, 0, 0))],
        out_specs=blk, out_shape=jax.ShapeDtypeStruct((t, d), BF16),
        compiler_params=_params(2))(q, k, v, logf_t.reshape(batch, heads // 2, 2, s_len))


def _logf_cumsum_kernel(lf_ref, cum_ref):
    cum_ref[0] = _lane_cumsum(lf_ref[0]) * LOG2E


def _logf_cumsum(logf_t):
    blk = pl.BlockSpec((1,) + logf_t.shape[1:], lambda b: (b, 0, 0))
    return pl.pallas_call(_logf_cumsum_kernel, grid=(logf_t.shape[0],), in_specs=[blk], out_specs=blk,
                          out_shape=jax.ShapeDtypeStruct(logf_t.shape, F32),
                          compiler_params=_params(1))(logf_t)


FOX_SAMPLE_KEYS = 1024


def _fox_sample_kernel(q_ref, k_ref, v_ref, ck_ref, cv_ref, cc_ref, cn_ref, o_ref,
                       m_ref, l_ref, acc_ref, *, heads):
    n_new, d = q_ref.shape
    head_dim = d // heads
    kt = pl.program_id(1)
    cols = [slice(hd * head_dim, (hd + 1) * head_dim) for hd in range(heads)]
    rows = [slice(hd * n_new, (hd + 1) * n_new) for hd in range(heads)]

    @pl.when(kt == 0)
    def _():
        m_ref[...] = jnp.full(m_ref.shape, NEG_INF, F32)
        l_ref[...] = jnp.zeros(l_ref.shape, F32)
        acc_ref[...] = jnp.zeros(acc_ref.shape, F32)

    def update(s, head_pv):
        m_old = m_ref[...]
        m_new = jnp.maximum(m_old, jnp.max(s, axis=-1, keepdims=True))
        alpha = jnp.exp2(m_old - m_new)
        p = jnp.exp2(s - m_new)
        l_ref[...] = alpha * l_ref[...] + jnp.sum(p, axis=-1, keepdims=True)
        p = p.astype(BF16)
        acc_ref[...] = alpha * acc_ref[...] + _rows([head_pv(hd, p[r]) for hd, r in enumerate(rows)])
        m_ref[...] = m_new

    update(_rows([_dot(q_ref[:, c], ck_ref[0, c, :].astype(BF16)) - cc_ref[0, hd:hd + 1, :]
                  for hd, c in enumerate(cols)]),
           lambda hd, p: _dot_nt(p, cv_ref[0, cols[hd], :].astype(BF16)))

    @pl.when(kt == pl.num_programs(1) - 1)
    def _():
        causal = (lax.broadcasted_iota(jnp.int32, (n_new, n_new), 1)
                  <= lax.broadcasted_iota(jnp.int32, (n_new, n_new), 0))
        s = _rows([jnp.where(causal, _dot_nt(q_ref[:, c], k_ref[:, c]) - cn_ref[0, hd:hd + 1, 0:n_new],
                             NEG_INF) for hd, c in enumerate(cols)])
        update(s, lambda hd, p: _dot(p, v_ref[:, cols[hd]]))
        o = acc_ref[...] * (1.0 / l_ref[...])
        for hd, c in enumerate(cols):
            o_ref[:, c] = o[rows[hd]].astype(BF16)


def _fox_sample(q, k, v, cache_k, cache_v, cum, *, batch):
    t, d = q.shape
    n_new = t // batch
    _, past, heads, head_dim = cache_k.shape
    tk = min(FOX_SAMPLE_KEYS, past)
    assert past % tk == 0 and past % LANES == 0 and tk % LANES == 0
    row = pl.BlockSpec((n_new, d), lambda b, kt: (b, 0))
    cache = pl.BlockSpec((1, d, tk), lambda b, kt: (b, 0, kt))
    kern = functools.partial(_fox_sample_kernel, heads=heads)
    return pl.pallas_call(
        kern, grid=(batch, past // tk),
        in_specs=[row, row, row, cache, cache,
                  pl.BlockSpec((1, heads, tk), lambda b, kt: (b, 0, kt)),
                  pl.BlockSpec((1, heads, LANES), lambda b, kt: (b, 0, past // LANES))],
        out_specs=row, out_shape=jax.ShapeDtypeStruct((t, d), BF16),
        scratch_shapes=[pltpu.VMEM((heads * n_new, 1), F32), pltpu.VMEM((heads * n_new, 1), F32),
                        pltpu.VMEM((heads * n_new, head_dim), F32)],
        compiler_params=_params(2))(q, k, v, _feature_major_cache(cache_k),
                                    _feature_major_cache(cache_v), cum, cum)


RET_BLOCK = 256


def _ret_proj_kernel(h_ref, w_ref, cos_ref, sin_ref, q_ref, k_ref, v_ref, g_ref, *,
                     heads, key_dim, val_width):
    h = h_ref[...]
    cos, sin = cos_ref[...], sin_ref[...]
    half = key_dim // 2
    qk_width = heads * key_dim
    k_scale = key_dim ** -0.5
    for part, (ref, scale) in enumerate(((q_ref, 1.0), (k_ref, k_scale))):
        for hd in range(heads):
            c0 = part * qk_width + hd * key_dim
            x = _dot(h, w_ref[:, c0:c0 + key_dim])
            x1, x2 = x[:, :half], x[:, half:]
            o0 = hd * key_dim
            ref[:, o0:o0 + half] = ((x1 * cos - x2 * sin) * scale).astype(BF16)
            ref[:, o0 + half:o0 + key_dim] = ((x2 * cos + x1 * sin) * scale).astype(BF16)
    v0 = 2 * qk_width
    step = 4 * MXU_DIM
    for c in range(0, val_width, step):
        v_ref[:, c:c + step] = _dot(h, w_ref[:, v0 + c:v0 + c + step]).astype(BF16)
        g = _dot(h, w_ref[:, v0 + val_width + c:v0 + val_width + c + step])
        g_ref[:, c:c + step] = (g * jax.nn.sigmoid(g)).astype(BF16)


def _ret_proj(h, w, cos, sin, *, heads, tm):
    t, d = h.shape
    tm = min(tm, t, cos.shape[0])
    key_dim = d // heads
    val_width = 2 * d
    n_period = cos.shape[0] // tm
    row = lambda i: (i, 0)
    tab = pl.BlockSpec((tm, key_dim // 2), lambda i: (i % n_period, 0))
    kern = functools.partial(_ret_proj_kernel, heads=heads, key_dim=key_dim, val_width=val_width)
    return pl.pallas_call(
        kern, grid=(t // tm,),
        in_specs=[pl.BlockSpec((tm, d), row), _resident(w.shape), tab, tab],
        out_specs=[pl.BlockSpec((tm, d), row), pl.BlockSpec((tm, d), row),
                   pl.BlockSpec((tm, val_width), row), pl.BlockSpec((tm, val_width), row)],
        out_shape=[jax.ShapeDtypeStruct((t, d), BF16), jax.ShapeDtypeStruct((t, d), BF16),
                   jax.ShapeDtypeStruct((t, val_width), BF16),
                   jax.ShapeDtypeStruct((t, val_width), BF16)],
        compiler_params=_params(1))(h, w.astype(BF16), cos, sin)


def _retention_kernel(*refs, n_blocks, blk, has_init):
    it = iter(refs)
    q_ref, k_ref, v_ref, g_ref, gn_ref = (next(it) for _ in range(5))
    dec_ref, qd_ref, kd_ref, sd_ref = (next(it) for _ in range(4))
    s0_ref = next(it) if has_init else None
    y_ref, so_ref, st_ref = next(it), next(it), next(it)

    if has_init:
        st_ref[...] = s0_ref[0, 0]
    else:
        st_ref[...] = jnp.zeros_like(st_ref)
    decay = dec_ref[0]
    q_decay, k_decay, s_decay = qd_ref[0], kd_ref[0], sd_ref[0]
    for c in range(n_blocks):
        rows = slice(c * blk, (c + 1) * blk)
        q, k, v = q_ref[rows, :], k_ref[rows, :], v_ref[rows, :]
        state = st_ref[...]
        scores = (_dot_nt(q, k) * decay).astype(BF16)
        o = _dot(scores, v) + _dot(q, state.astype(BF16)) * q_decay
        k_dec = (k.astype(F32) * k_decay).astype(BF16)
        st_ref[...] = s_decay * state + _dot_tn(k_dec, v)
        mu = jnp.mean(o, axis=-1, keepdims=True)
        cen = o - mu
        var = jnp.mean(cen * cen, axis=-1, keepdims=True)
        y = g_ref[rows, :].astype(F32) * (cen * lax.rsqrt(var + GN_EPS) * gn_ref[...])
        y_ref[rows, :] = y.astype(BF16)
    so_ref[0, 0] = st_ref[...]


def _retention_decays(heads, blk):
    log_gamma = np.log1p(-np.exp2(-5.0 - np.arange(heads, dtype=np.float64)))
    idx = np.arange(blk, dtype=np.float64)
    diff = idx[:, None] - idx[None, :]
    decay = np.where(diff >= 0, np.exp(np.maximum(diff, 0.0)[None] * log_gamma[:, None, None]), 0.0)
    q_decay = np.exp((idx + 1.0)[None, :] * log_gamma[:, None])[..., None]
    k_decay = np.exp((blk - 1.0 - idx)[None, :] * log_gamma[:, None])[..., None]
    s_decay = np.exp(blk * log_gamma)[:, None, None]
    return tuple(jnp.asarray(a, F32) for a in (decay, q_decay, k_decay, s_decay))


def _retention(q, k, v, gate, gn_g, state0, *, batch, heads):
    t, d = q.shape
    s_len = t // batch
    key_dim = d // heads
    val_dim = v.shape[1] // heads
    blk = min(RET_BLOCK, s_len)
    decay, q_decay, k_decay, s_decay = _retention_decays(heads, blk)
    qk_blk = pl.BlockSpec((s_len, key_dim), lambda b, h: (b, h))
    v_blk = pl.BlockSpec((s_len, val_dim), lambda b, h: (b, h))
    per_head = lambda shape: pl.BlockSpec((1,) + shape, lambda b, h: (h, 0, 0))
    state_blk = pl.BlockSpec((1, 1, key_dim, val_dim), lambda b, h: (b, h, 0, 0))
    in_specs = [qk_blk, qk_blk, v_blk, v_blk, pl.BlockSpec((1, val_dim), lambda b, h: (0, h)),
                per_head((blk, blk)), per_head((blk, 1)), per_head((blk, 1)), per_head((1, 1))]
    args = [q, k, v, gate, gn_g.reshape(1, -1).astype(F32), decay, q_decay, k_decay, s_decay]
    if state0 is not None:
        in_specs.append(state_blk)
        args.append(state0)
    kern = functools.partial(_retention_kernel, n_blocks=s_len // blk, blk=blk,
                             has_init=state0 is not None)
    return pl.pallas_call(
        kern, grid=(batch, heads), in_specs=in_specs, out_specs=[v_blk, state_blk],
        out_shape=[jax.ShapeDtypeStruct(v.shape, BF16),
                   jax.ShapeDtypeStruct((batch, heads, key_dim, val_dim), F32)],
        scratch_shapes=[pltpu.VMEM((key_dim, val_dim), F32)],
        compiler_params=_params(2))(*args)


def _rotary_tables(pos, half):
    inv_freq = ROPE_BASE ** (-jnp.arange(half, dtype=F32) / half)
    ang = pos.astype(F32)[:, None] * inv_freq[None, :]
    return jnp.cos(ang), jnp.sin(ang)


TM_PROMPT = 512
TM_FFN = 1024


def kernel(x_prompt, x_sample, cache_chunk_k, cache_chunk_v, cache_fox_k, cache_fox_v, cache_fox_logf,
           state_ret, norm_g, w_ffn_in, w_ffn_out, a_w_in, a_g_q, a_g_k, a_rel_table, a_w_out,
           b_w_in, b_bias_f, b_g_q, b_g_k, b_w_out, c_w_in, c_gn_g, c_w_out):
    batch, seq, d = x_prompt.shape
    dec_batch, dec_seq, _ = x_sample.shape
    past = cache_fox_k.shape[2]
    depth = norm_g.shape[0]
    xp = x_prompt.reshape(batch * seq, d)
    xs = x_sample.reshape(dec_batch * dec_seq, d)
    head_dim = d // A_HEADS
    outs = {name: [] for name in ("a_kp", "a_vp", "a_ks", "a_vs", "b_kp", "b_vp", "b_fp",
                                  "b_ks", "b_vs", "b_fs", "c_sp", "c_ss")}

    for i in range(depth):
        g = norm_g[i].astype(F32)
        kind, j = i % N_MIXERS, i // N_MIXERS
        w1_in, w1_out = w_ffn_in[i, 0].astype(BF16), w_ffn_out[i, 0].astype(BF16)
        w2_in, w2_out = w_ffn_in[i, 1].astype(BF16), w_ffn_out[i, 1].astype(BF16)
        gains1 = jnp.stack([g[0], g[0], g[1]])
        gains2 = jnp.stack([g[2], g[3], g[3]])
        xp, hp = _ffn(xp, gains1, w1_in, w1_out, has_post=False, has_hout=True, tm=TM_FFN)
        xs, hs = _ffn(xs, gains1, w1_in, w1_out, has_post=False, has_hout=True, tm=TM_FFN)

        if kind == 0:
            keep = min(A_WINDOW, seq)
            qp, kbp, vbp, kp, vp = _qkv_proj(hp, a_w_in[j], a_g_q[j], a_g_k[j], heads=A_HEADS,
                                             seq=seq, keep=keep, tm=TM_PROMPT)
            qs, kbs, vbs, ks, vs = _qkv_proj(hs, a_w_in[j], a_g_q[j], a_g_k[j], heads=A_HEADS,
                                             tm=TM_PROMPT)
            op = _band_prompt(qp, kbp, vbp, a_rel_table[j], batch=batch, tq=256)
            os_ = _band_sample(qs, kbs, vbs, cache_chunk_k[j], cache_chunk_v[j], a_rel_table[j],
                               batch=dec_batch)
            outs["a_kp"].append(_token_major_heads(kp, A_HEADS))
            outs["a_vp"].append(_token_major_heads(vp, A_HEADS))
            outs["a_ks"].append(ks.reshape(dec_batch, dec_seq, A_HEADS, head_dim))
            outs["a_vs"].append(vs.reshape(dec_batch, dec_seq, A_HEADS, head_dim))
            w_o = a_w_out[j].astype(BF16)
        elif kind == 1:
            forget = (b_w_in[j][:, 3 * d:], b_bias_f[j])
            qp, kbp, vbp, kp, vp, fpt = _qkv_proj(hp, b_w_in[j], b_g_q[j], b_g_k[j], heads=B_HEADS,
                                                  seq=seq, forget=forget, tm=TM_PROMPT)
            qs, kbs, vbs, ks, vs, fs = _qkv_proj(hs, b_w_in[j], b_g_q[j], b_g_k[j], heads=B_HEADS,
                                                 forget=forget, tm=TM_PROMPT)
            op = _fox_prompt(qp, kbp, vbp, fpt, batch=batch, tq=256)
            total = past + dec_seq
            padded = -(-total // LANES) * LANES
            lf_all = jnp.concatenate([cache_fox_logf[j].astype(F32),
                                      fs.reshape(dec_batch, dec_seq, B_HEADS)], axis=1)
            lf_all = jnp.pad(jnp.swapaxes(lf_all, 1, 2), ((0, 0), (0, 0), (0, padded - total)))
            os_ = _fox_sample(qs, kbs, vbs, cache_fox_k[j], cache_fox_v[j], _logf_cumsum(lf_all),
                              batch=dec_batch)
            outs["b_kp"].append(_token_major_heads(kp, B_HEADS))
            outs["b_vp"].append(_token_major_heads(vp, B_HEADS))
            outs["b_fp"].append(jnp.swapaxes(fpt, 1, 2))
            outs["b_ks"].append(ks.reshape(dec_batch, dec_seq, B_HEADS, head_dim))
            outs["b_vs"].append(vs.reshape(dec_batch, dec_seq, B_HEADS, head_dim))
            outs["b_fs"].append(fs.reshape(dec_batch, dec_seq, B_HEADS))
            w_o = b_w_out[j].astype(BF16)
        else:
            half = d // C_HEADS // 2
            cos_p, sin_p = _rotary_tables(jnp.arange(seq), half)
            cos_s, sin_s = _rotary_tables(past + jnp.arange(dec_seq), half)
            cos_s, sin_s = jnp.tile(cos_s, (dec_batch, 1)), jnp.tile(sin_s, (dec_batch, 1))
            qp, kp, vp, gp = _ret_proj(hp, c_w_in[j], cos_p, sin_p, heads=C_HEADS, tm=TM_PROMPT)
            qs, ks, vs, gs = _ret_proj(hs, c_w_in[j], cos_s, sin_s, heads=C_HEADS, tm=TM_PROMPT)
            op, sp = _retention(qp, kp, vp, gp, c_gn_g[j], None, batch=batch, heads=C_HEADS)
            os_, ss = _retention(qs, ks, vs, gs, c_gn_g[j], state_ret[j].astype(F32),
                                 batch=dec_batch, heads=C_HEADS)
            outs["c_sp"].append(sp)
            outs["c_ss"].append(ss)
            w_o = c_w_out[j].astype(BF16)

        xp = _ffn(xp, gains2, w2_in, w2_out, mix=(op, w_o), has_post=True, has_hout=False,
                  tm=TM_FFN)
        xs = _ffn(xs, gains2, w2_in, w2_out, mix=(os_, w_o), has_post=True, has_hout=False,
                  tm=TM_FFN)

    stack = lambda name: jnp.stack(outs[name])
    return (xp.reshape(batch, seq, d), xs.reshape(dec_batch, dec_seq, d),
            stack("a_kp"), stack("a_vp"), stack("a_ks"), stack("a_vs"),
            stack("b_kp"), stack("b_vp"), stack("b_fp"), stack("b_ks"), stack("b_vs"), stack("b_fs"),
            stack("c_sp"), stack("c_ss"))
```

```python
import functools

import numpy as np
import jax
import jax.numpy as jnp
from jax import lax
from jax.experimental import pallas as pl
from jax.experimental.pallas import tpu as pltpu

CHUNK = 64
A_HEADS = 16
A_LEFT_CHUNKS = 8
A_WINDOW = A_LEFT_CHUNKS * CHUNK
A_BAND = A_WINDOW + CHUNK
A_REL_CLIP = 128
B_HEADS = 16
C_HEADS = 4
ROPE_BASE = 10000.0
NORM_EPS = 1e-6
GN_EPS = 1e-5
NEG_INF = -1e30
N_MIXERS = 3
LOG2E = 1.4426950408889634

LANES = 128
HEAD_PAIR = LANES
MXU_DIM = 256
VMEM_LIMIT = 56 * 1024 * 1024

BF16 = jnp.bfloat16
F32 = jnp.float32


def _params(n_axes, vmem=VMEM_LIMIT):
    return pltpu.CompilerParams(dimension_semantics=("arbitrary",) * n_axes,
                                vmem_limit_bytes=vmem)


def _resident(shape):
    nd = len(shape)
    return pl.BlockSpec(shape, lambda *_: (0,) * nd, pipeline_mode=pl.Buffered(1))


def _dot(a, b):
    return jnp.dot(a, b, preferred_element_type=F32)


def _dot_nt(a, b):
    return lax.dot_general(a, b, (((1,), (1,)), ((), ())), preferred_element_type=F32)


def _dot_tn(a, b):
    return lax.dot_general(a, b, (((0,), (0,)), ((), ())), preferred_element_type=F32)


def _rms(x, g):
    return x * lax.rsqrt(jnp.mean(x * x, axis=-1, keepdims=True) + NORM_EPS) * g


def _ffn_kernel(*refs, has_mix, has_post, has_hout, bounds, d_ff, sub_rows):
    it = iter(refs)
    x_ref = next(it)
    if has_mix:
        o_ref, wo_ref = next(it), next(it)
    g_ref, win_ref, wout_ref = next(it), next(it), next(it)
    xo_ref = next(it)
    ho_ref = next(it) if has_hout else None

    for r in range(0, x_ref.shape[0], sub_rows):
        rows = slice(r, r + sub_rows)
        x = x_ref[rows, :]
        if has_mix:
            x = x + _dot(o_ref[rows, :], wo_ref[...])
        h = _rms(x, g_ref[0:1, :]).astype(BF16)
        y = None
        for a, b in bounds:
            gate = _dot(h, win_ref[:, a:b])
            up = _dot(h, win_ref[:, d_ff + a:d_ff + b])
            act = (gate * jax.nn.sigmoid(gate) * up).astype(BF16)
            part = _dot(act, wout_ref[a:b, :])
            y = part if y is None else y + part
        x = x + 0.5 * y
        if has_post:
            x = _rms(x, g_ref[1:2, :])
        xo_ref[rows, :] = x
        if has_hout:
            ho_ref[rows, :] = _rms(x, g_ref[2:3, :]).astype(BF16)


FFN_SUB_ROWS = MXU_DIM


def _ffn_bounds(d_ff):
    step = 4 * MXU_DIM
    return tuple((a, min(a + step, d_ff)) for a in range(0, d_ff, step))


def _ffn(x, gains, w_in, w_out, *, mix=None, has_post, has_hout, tm):
    t, d = x.shape
    d_ff = w_out.shape[0]
    tm = min(tm, t)
    row = lambda i: (i, 0)
    in_specs = [pl.BlockSpec((tm, d), row)]
    args = [x]
    if mix is not None:
        o, w_o = mix
        in_specs += [pl.BlockSpec((tm, o.shape[1]), row), _resident(w_o.shape)]
        args += [o, w_o]
    in_specs += [_resident(gains.shape), _resident(w_in.shape), _resident(w_out.shape)]
    args += [gains, w_in, w_out]
    out_shape = [jax.ShapeDtypeStruct((t, d), F32)]
    out_specs = [pl.BlockSpec((tm, d), row)]
    if has_hout:
        out_shape.append(jax.ShapeDtypeStruct((t, d), BF16))
        out_specs.append(pl.BlockSpec((tm, d), row))
    kern = functools.partial(_ffn_kernel, has_mix=mix is not None, has_post=has_post,
                             has_hout=has_hout, bounds=_ffn_bounds(d_ff), d_ff=d_ff,
                             sub_rows=min(FFN_SUB_ROWS, tm))
    res = pl.pallas_call(kern, grid=(t // tm,), in_specs=in_specs, out_specs=out_specs,
                         out_shape=out_shape, compiler_params=_params(1))(*args)
    return res if has_hout else res[0]


def _head_mean_square(x, e_ref, head_dim):
    sq = (x * x).astype(BF16)
    parts = [_dot(sq[:, c:c + MXU_DIM], e_ref[...]) for c in range(0, x.shape[1], MXU_DIM)]
    return jnp.concatenate(parts, axis=1) * (1.0 / head_dim)


def _log_sigmoid(x):
    return jnp.minimum(x, 0.0) - jnp.log1p(jnp.exp(-jnp.abs(x)))


PROJ_SUB_ROWS = MXU_DIM


def _qkv_kernel(*refs, d, head_dim, has_f, feature_major, tail_tiles, sub_rows):
    it = iter(refs)
    h_ref, wq_ref, wk_ref, wv_ref, e_ref, gq_ref, gk_ref = (next(it) for _ in range(7))
    if has_f:
        wf_ref, bf_ref = next(it), next(it)
    q_ref, kb_ref, vb_ref, kf_ref, vf_ref = (next(it) for _ in range(5))
    lf_ref = next(it) if has_f else None
    heads = d // head_dim

    kept_f32 = []
    for r0 in range(0, h_ref.shape[0], sub_rows):
        rows = slice(r0, r0 + sub_rows)
        h = h_ref[rows, :]
        if feature_major:
            k = _dot_nt(wk_ref[...], h).reshape(heads, head_dim, sub_rows)
            ms = jnp.mean(k * k, axis=1, keepdims=True)
            k = (k * lax.rsqrt(ms + NORM_EPS)).reshape(d, sub_rows) * gk_ref[...]
            v = _dot_nt(wv_ref[...], h)
            out = (0, slice(None), rows)
        else:
            k = _dot(h, wk_ref[...])
            k = k * lax.rsqrt(_head_mean_square(k, e_ref, head_dim) + NORM_EPS) * gk_ref[...]
            v = _dot(h, wv_ref[...])
            out = (rows, slice(None))
        kb_ref[out] = k.astype(BF16)
        vb_ref[out] = v.astype(BF16)
        if tail_tiles is None:
            kf_ref[out] = k
            vf_ref[out] = v
        else:
            kept_f32.append((out, k, v))
        q = _dot(h, wq_ref[...])
        q = q * lax.rsqrt(_head_mean_square(q, e_ref, head_dim) + NORM_EPS) * gq_ref[...]
        q_ref[rows, :] = q.astype(BF16)
        if has_f:
            if feature_major:
                lf_ref[out] = _log_sigmoid(_dot_nt(wf_ref[...], h) + bf_ref[...])
            else:
                lf_ref[out] = _log_sigmoid(_dot(h, wf_ref[...]) + bf_ref[...])

    if tail_tiles is not None:
        per_seq, kept = tail_tiles

        @pl.when(pl.program_id(0) % per_seq >= per_seq - kept)
        def _():
            for out, k, v in kept_f32:
                kf_ref[out] = k
                vf_ref[out] = v


def _qkv_proj(h, w, g_q, g_k, *, heads, seq=None, keep=None, forget=None, tm):
    t, d = h.shape
    tm = min(tm, t)
    head_dim = d // heads
    scale = head_dim ** -0.5 * LOG2E
    blk = np.kron(np.eye(MXU_DIM // head_dim), np.ones((head_dim, head_dim)))
    e = jnp.asarray(blk, BF16)
    gq = (jnp.tile(g_q.astype(F32), heads) * scale).reshape(1, d)
    gk = jnp.tile(g_k.astype(F32), heads)
    wq, wk, wv = (w[:, c * d:(c + 1) * d].astype(BF16) for c in range(3))
    row = lambda i: (i, 0)
    tail_tiles = None
    if seq is None:
        gk = gk.reshape(1, d)
        kv_shape, kv_spec = (t, d), pl.BlockSpec((tm, d), row)
        f32_shape, f32_spec = kv_shape, kv_spec
    else:
        keep = seq if keep is None else keep
        assert seq % tm == 0 and keep % tm == 0
        per_seq, kept = seq // tm, keep // tm
        gk, wk, wv = gk.reshape(d, 1), wk.T, wv.T
        kv_shape = (t // seq, d, seq)
        kv_spec = pl.BlockSpec((1, d, tm), lambda i: (i // per_seq, 0, i % per_seq))
        f32_shape, f32_spec = kv_shape, kv_spec
        if kept != per_seq:
            tail_tiles = (per_seq, kept)
            f32_shape = (t // seq, d, keep)
            f32_spec = pl.BlockSpec((1, d, tm), lambda i: (i // per_seq, 0,
                                                           jnp.maximum(i % per_seq - (per_seq - kept), 0)))
    in_specs = [pl.BlockSpec((tm, d), row)] + [_resident((d, d))] * 3 + [
        _resident(e.shape), _resident(gq.shape), _resident(gk.shape)]
    args = [h, wq, wk, wv, e, gq, gk]
    out_shape = ([jax.ShapeDtypeStruct((t, d), BF16)] + [jax.ShapeDtypeStruct(kv_shape, BF16)] * 2
                 + [jax.ShapeDtypeStruct(f32_shape, F32)] * 2)
    out_specs = [pl.BlockSpec((tm, d), row), kv_spec, kv_spec, f32_spec, f32_spec]
    if forget is not None:
        w_f, bias_f = forget
        if seq is None:
            w_f, bias_f = w_f.astype(BF16), bias_f.reshape(1, heads).astype(F32)
            out_shape.append(jax.ShapeDtypeStruct((t, heads), F32))
            out_specs.append(pl.BlockSpec((tm, heads), row))
        else:
            w_f, bias_f = w_f.T.astype(BF16), bias_f.reshape(heads, 1).astype(F32)
            out_shape.append(jax.ShapeDtypeStruct((t // seq, heads, seq), F32))
            out_specs.append(pl.BlockSpec((1, heads, tm), kv_spec.index_map))
        args += [w_f, bias_f]
        in_specs += [_resident(w_f.shape), _resident(bias_f.shape)]
    kern = functools.partial(_qkv_kernel, d=d, head_dim=head_dim, has_f=forget is not None,
                             feature_major=seq is not None, tail_tiles=tail_tiles,
                             sub_rows=min(PROJ_SUB_ROWS, tm))
    return pl.pallas_call(kern, grid=(t // tm,), in_specs=in_specs, out_specs=out_specs,
                          out_shape=out_shape, compiler_params=_params(1))(*args)


def _head_masks(shape):
    lane = lax.broadcasted_iota(jnp.int32, shape, len(shape) - 1)
    first = lane < (HEAD_PAIR // 2)
    return first, jnp.logical_not(first)


def _softmax_pv(score_parts, value_parts):
    m = None
    for s in score_parts:
        mx = jnp.max(s, axis=-1, keepdims=True)
        m = mx if m is None else jnp.maximum(m, mx)
    acc, l = None, None
    for s, v in zip(score_parts, value_parts):
        p = jnp.exp2(s - m)
        ls = jnp.sum(p, axis=-1, keepdims=True)
        pv = _dot(p.astype(BF16), v)
        acc = pv if acc is None else acc + pv
        l = ls if l is None else l + ls
    return acc * (1.0 / l)


def _stack_heads(q, first, second):
    zero = jnp.zeros_like(q)
    return jnp.concatenate([jnp.where(first, q, zero), jnp.where(second, q, zero)], axis=0)


def _pv_normalised(p, vt, first, tq):
    ones = jnp.ones(vt.shape, vt.dtype)
    o2 = _dot_nt(p, jnp.concatenate([vt, ones], axis=0))
    o2 = o2[:, :HEAD_PAIR] * (1.0 / o2[:, HEAD_PAIR:])
    return jnp.where(first, o2[:tq], o2[tq:])


def _toeplitz_bias(rel_table, n_rows, n_cols, offset):
    n_diag = n_rows + n_cols - 1
    u = np.arange(n_diag) - (n_rows - 1)
    idx = np.clip(offset - u, -A_REL_CLIP, A_REL_CLIP) + A_REL_CLIP
    diag = jnp.pad(rel_table.astype(F32)[:, idx], ((0, 0), (0, 1)))
    heads = diag.shape[0]
    skew = jnp.tile(diag, (1, n_rows))[:, :n_rows * n_diag].reshape(heads, n_rows, n_diag)
    return skew[:, :, n_rows - 1:n_rows - 1 + n_cols]


def _band_prompt_kernel(q_ref, k_ref, v_ref, bm_ref, o_ref, *, tq, window):
    s_len = q_ref.shape[0]
    first, second = _head_masks((tq, HEAD_PAIR))
    width = window + tq
    for t in range(s_len // tq):
        lo, hi = max(0, t * tq - window), (t + 1) * tq
        qs = _stack_heads(q_ref[t * tq:hi, :], first, second)
        s = _dot(qs, k_ref[0, :, lo:hi]) + bm_ref[0, :, width - (hi - lo):]
        p = jnp.exp2(s - jnp.max(s, axis=-1, keepdims=True)).astype(BF16)
        o_ref[t * tq:hi, :] = _pv_normalised(p, v_ref[0, :, lo:hi], first, tq).astype(BF16)


def _band_bias_mask(rel_table, tq):
    width = A_WINDOW + tq
    chunk_bias = _toeplitz_bias(rel_table, CHUNK, A_BAND, A_BAND - CHUNK) * LOG2E
    rows = [jnp.pad(chunk_bias, ((0, 0), (0, 0), (c * CHUNK, width - A_BAND - c * CHUNK)),
                    constant_values=NEG_INF) for c in range(tq // CHUNK)]
    bm = jnp.concatenate(rows, axis=1)
    return bm.reshape(bm.shape[0] // 2, 2 * tq, width)


def _band_prompt(q, k, v, rel_table, *, batch, tq):
    t, d = q.shape
    s_len = t // batch
    tq = min(tq, s_len)
    bm = _band_bias_mask(rel_table, tq)
    blk = pl.BlockSpec((s_len, HEAD_PAIR), lambda b, hp: (b, hp))
    kv_blk = pl.BlockSpec((1, HEAD_PAIR, s_len), lambda b, hp: (b, hp, 0))
    kern = functools.partial(_band_prompt_kernel, tq=tq, window=A_WINDOW)
    return pl.pallas_call(
        kern, grid=(batch, d // HEAD_PAIR),
        in_specs=[blk, kv_blk, kv_blk, pl.BlockSpec((1,) + bm.shape[1:], lambda b, hp: (hp, 0, 0))],
        out_specs=blk, out_shape=jax.ShapeDtypeStruct((t, d), BF16),
        compiler_params=_params(2))(q, k, v, bm)


def _rows(parts):
    return jnp.concatenate(parts, axis=0)


def _feature_major_cache(cache):
    *lead, n, heads, head_dim = cache.shape
    nl = len(lead)
    perm = tuple(range(nl)) + (nl + 1, nl + 2, nl)
    return jnp.transpose(cache, perm).reshape(*lead, heads * head_dim, n)


def _token_major_heads(x_t, heads):
    b, d, n = x_t.shape
    return jnp.transpose(x_t.reshape(b, heads, d // heads, n), (0, 3, 1, 2))


def _band_sample_kernel(q_ref, k_ref, v_ref, ck_ref, cv_ref, bias_ref, o_ref, *, heads):
    n_new, d = q_ref.shape
    head_dim = d // heads
    win = ck_ref.shape[-1]
    cols = [slice(hd * head_dim, (hd + 1) * head_dim) for hd in range(heads)]
    s_c = _rows([_dot(q_ref[:, c], ck_ref[0, 0, c, :].astype(BF16)) for c in cols])
    s_n = _rows([_dot_nt(q_ref[:, c], k_ref[:, c]) for c in cols])
    s_c = s_c + bias_ref[:, :win]
    s_n = s_n + bias_ref[:, win:]
    m = jnp.maximum(jnp.max(s_c, axis=-1, keepdims=True), jnp.max(s_n, axis=-1, keepdims=True))
    p_c, p_n = jnp.exp2(s_c - m), jnp.exp2(s_n - m)
    inv_l = 1.0 / (jnp.sum(p_c, axis=-1, keepdims=True) + jnp.sum(p_n, axis=-1, keepdims=True))
    p_c, p_n = p_c.astype(BF16), p_n.astype(BF16)
    for hd, c in enumerate(cols):
        rows = slice(hd * n_new, (hd + 1) * n_new)
        o = _dot_nt(p_c[rows], cv_ref[0, 0, c, :].astype(BF16)) + _dot(p_n[rows], v_ref[:, c])
        o_ref[:, c] = (o * inv_l[rows]).astype(BF16)


def _band_sample(q, k, v, cache_k, cache_v, rel_table, *, layer, batch):
    t, d = q.shape
    n_new = t // batch
    _, _, win, heads, _ = cache_k.shape
    bias = (_toeplitz_bias(rel_table, n_new, win + n_new, win) * LOG2E).reshape(heads * n_new, win + n_new)
    row = pl.BlockSpec((n_new, d), lambda b: (b, 0))
    cache = pl.BlockSpec((1, 1, d, win), lambda b: (layer, b, 0, 0))
    kern = functools.partial(_band_sample_kernel, heads=heads)
    return pl.pallas_call(
        kern, grid=(batch,),
        in_specs=[row, row, row, cache, cache, _resident(bias.shape)],
        out_specs=row, out_shape=jax.ShapeDtypeStruct((t, d), BF16),
        compiler_params=_params(1))(q, k, v, _feature_major_cache(cache_k),
                                    _feature_major_cache(cache_v), bias)


def _lane_cumsum(x):
    n = x.shape[-1]
    lane = lax.broadcasted_iota(jnp.int32, x.shape, x.ndim - 1)
    shift = 1
    while shift < n:
        x = x + jnp.where(lane >= shift, pltpu.roll(x, shift, x.ndim - 1), 0.0)
        shift *= 2
    return x


def _fox_prompt_kernel(q_ref, k_ref, v_ref, lft_ref, o_ref, *, tq):
    s_len = q_ref.shape[0]
    cum = _lane_cumsum(lft_ref[0, 0]) * LOG2E
    first, second = _head_masks((tq, HEAD_PAIR))
    r = lax.broadcasted_iota(jnp.int32, (2 * tq, tq), 0)
    c = lax.broadcasted_iota(jnp.int32, (2 * tq, tq), 1)
    causal = c <= jnp.where(r >= tq, r - tq, r)
    n_tiles = s_len // tq

    def scores(t):
        qs = _stack_heads(q_ref[t * tq:(t + 1) * tq, :], first, second)
        return _dot(qs, k_ref[0, :, 0:(t + 1) * tq])

    s_next = scores(0)
    for t in range(n_tiles):
        lo, hi = t * tq, (t + 1) * tq
        s = s_next
        if t + 1 < n_tiles:
            s_next = scores(t + 1)
        s = jnp.concatenate([s[:tq] - cum[0:1, 0:hi], s[tq:] - cum[1:2, 0:hi]], axis=0)
        parts = [jnp.where(causal, s[:, lo:hi], NEG_INF)]
        m = jnp.max(parts[0], axis=-1, keepdims=True)
        if t > 0:
            parts.insert(0, s[:, 0:lo])
            m = jnp.maximum(m, jnp.max(parts[0], axis=-1, keepdims=True))
        p = jnp.concatenate([jnp.exp2(x - m) for x in parts], axis=1).astype(BF16)
        o_ref[lo:hi, :] = _pv_normalised(p, v_ref[0, :, 0:hi], first, tq).astype(BF16)


def _fox_prompt(q, k, v, logf_t, *, batch, tq):
    t, d = q.shape
    s_len = t // batch
    tq = min(tq, s_len)
    heads = logf_t.shape[1]
    blk = pl.BlockSpec((s_len, HEAD_PAIR), lambda b, hp: (b, hp))
    kv_blk = pl.BlockSpec((1, HEAD_PAIR, s_len), lambda b, hp: (b, hp, 0))
    kern = functools.partial(_fox_prompt_kernel, tq=tq)
    return pl.pallas_call(
        kern, grid=(batch, d // HEAD_PAIR),
        in_specs=[blk, kv_blk, kv_blk, pl.BlockSpec((1, 1, 2, s_len), lambda b, hp: (b, hp, 0, 0))],
        out_specs=blk, out_shape=jax.ShapeDtypeStruct((t, d), BF16),
        compiler_params=_params(2))(q, k, v, logf_t.reshape(batch, heads // 2, 2, s_len))


def _logf_cumsum_kernel(lf_ref, cum_ref):
    cum_ref[0] = _lane_cumsum(lf_ref[0]) * LOG2E


def _logf_cumsum(logf_t):
    blk = pl.BlockSpec((1,) + logf_t.shape[1:], lambda b: (b, 0, 0))
    return pl.pallas_call(_logf_cumsum_kernel, grid=(logf_t.shape[0],), in_specs=[blk], out_specs=blk,
                          out_shape=jax.ShapeDtypeStruct(logf_t.shape, F32),
                          compiler_params=_params(1))(logf_t)


FOX_SAMPLE_KEYS = 1024


def _fox_sample_kernel(q_ref, k_ref, v_ref, ck_ref, cv_ref, cc_ref, cn_ref, o_ref,
                       m_ref, l_ref, acc_ref, *, heads):
    n_new, d = q_ref.shape
    head_dim = d // heads
    kt = pl.program_id(1)
    cols = [slice(hd * head_dim, (hd + 1) * head_dim) for hd in range(heads)]
    rows = [slice(hd * n_new, (hd + 1) * n_new) for hd in range(heads)]

    @pl.when(kt == 0)
    def _():
        m_ref[...] = jnp.full(m_ref.shape, NEG_INF, F32)
        l_ref[...] = jnp.zeros(l_ref.shape, F32)
        acc_ref[...] = jnp.zeros(acc_ref.shape, F32)

    def update(s, head_pv):
        m_old = m_ref[...]
        m_new = jnp.maximum(m_old, jnp.max(s, axis=-1, keepdims=True))
        alpha = jnp.exp2(m_old - m_new)
        p = jnp.exp2(s - m_new)
        l_ref[...] = alpha * l_ref[...] + jnp.sum(p, axis=-1, keepdims=True)
        p = p.astype(BF16)
        acc_ref[...] = alpha * acc_ref[...] + _rows([head_pv(hd, p[r]) for hd, r in enumerate(rows)])
        m_ref[...] = m_new

    update(_rows([_dot(q_ref[:, c], ck_ref[0, c, :].astype(BF16)) - cc_ref[0, hd:hd + 1, :]
                  for hd, c in enumerate(cols)]),
           lambda hd, p: _dot_nt(p, cv_ref[0, cols[hd], :].astype(BF16)))

    @pl.when(kt == pl.num_programs(1) - 1)
    def _():
        causal = (lax.broadcasted_iota(jnp.int32, (n_new, n_new), 1)
                  <= lax.broadcasted_iota(jnp.int32, (n_new, n_new), 0))
        s = _rows([jnp.where(causal, _dot_nt(q_ref[:, c], k_ref[:, c]) - cn_ref[0, hd:hd + 1, 0:n_new],
                             NEG_INF) for hd, c in enumerate(cols)])
        update(s, lambda hd, p: _dot(p, v_ref[:, cols[hd]]))
        o = acc_ref[...] * (1.0 / l_ref[...])
        for hd, c in enumerate(cols):
            o_ref[:, c] = o[rows[hd]].astype(BF16)


def _fox_sample(q, k, v, cache_k, cache_v, cum, *, batch):
    t, d = q.shape
    n_new = t // batch
    _, past, heads, head_dim = cache_k.shape
    tk = min(FOX_SAMPLE_KEYS, past)
    assert past % tk == 0 and past % LANES == 0 and tk % LANES == 0
    row = pl.BlockSpec((n_new, d), lambda b, kt: (b, 0))
    cache = pl.BlockSpec((1, d, tk), lambda b, kt: (b, 0, kt))
    kern = functools.partial(_fox_sample_kernel, heads=heads)
    return pl.pallas_call(
        kern, grid=(batch, past // tk),
        in_specs=[row, row, row, cache, cache,
                  pl.BlockSpec((1, heads, tk), lambda b, kt: (b, 0, kt)),
                  pl.BlockSpec((1, heads, LANES), lambda b, kt: (b, 0, past // LANES))],
        out_specs=row, out_shape=jax.ShapeDtypeStruct((t, d), BF16),
        scratch_shapes=[pltpu.VMEM((heads * n_new, 1), F32), pltpu.VMEM((heads * n_new, 1), F32),
                        pltpu.VMEM((heads * n_new, head_dim), F32)],
        compiler_params=_params(2))(q, k, v, _feature_major_cache(cache_k),
                                    _feature_major_cache(cache_v), cum, cum)


RET_BLOCK = 256


def _ret_proj_kernel(h_ref, w_ref, cos_ref, sin_ref, q_ref, k_ref, v_ref, g_ref, *,
                     heads, key_dim, val_width):
    h = h_ref[...]
    cos, sin = cos_ref[...], sin_ref[...]
    half = key_dim // 2
    qk_width = heads * key_dim
    k_scale = key_dim ** -0.5
    for part, (ref, scale) in enumerate(((q_ref, 1.0), (k_ref, k_scale))):
        for hd in range(heads):
            c0 = part * qk_width + hd * key_dim
            x = _dot(h, w_ref[:, c0:c0 + key_dim])
            x1, x2 = x[:, :half], x[:, half:]
            o0 = hd * key_dim
            ref[:, o0:o0 + half] = ((x1 * cos - x2 * sin) * scale).astype(BF16)
            ref[:, o0 + half:o0 + key_dim] = ((x2 * cos + x1 * sin) * scale).astype(BF16)
    v0 = 2 * qk_width
    step = 4 * MXU_DIM
    for c in range(0, val_width, step):
        v_ref[:, c:c + step] = _dot(h, w_ref[:, v0 + c:v0 + c + step]).astype(BF16)
        g = _dot(h, w_ref[:, v0 + val_width + c:v0 + val_width + c + step])
        g_ref[:, c:c + step] = (g * jax.nn.sigmoid(g)).astype(BF16)


def _ret_proj(h, w, cos, sin, *, heads, tm):
    t, d = h.shape
    tm = min(tm, t, cos.shape[0])
    key_dim = d // heads
    val_width = 2 * d
    n_period = cos.shape[0] // tm
    row = lambda i: (i, 0)
    tab = pl.BlockSpec((tm, key_dim // 2), lambda i: (i % n_period, 0))
    kern = functools.partial(_ret_proj_kernel, heads=heads, key_dim=key_dim, val_width=val_width)
    return pl.pallas_call(
        kern, grid=(t // tm,),
        in_specs=[pl.BlockSpec((tm, d), row), _resident(w.shape), tab, tab],
        out_specs=[pl.BlockSpec((tm, d), row), pl.BlockSpec((tm, d), row),
                   pl.BlockSpec((tm, val_width), row), pl.BlockSpec((tm, val_width), row)],
        out_shape=[jax.ShapeDtypeStruct((t, d), BF16), jax.ShapeDtypeStruct((t, d), BF16),
                   jax.ShapeDtypeStruct((t, val_width), BF16),
                   jax.ShapeDtypeStruct((t, val_width), BF16)],
        compiler_params=_params(1))(h, w.astype(BF16), cos, sin)


def _retention_kernel(*refs, n_blocks, blk, has_init):
    it = iter(refs)
    q_ref, k_ref, v_ref, g_ref, gn_ref = (next(it) for _ in range(5))
    dec_ref, qd_ref, kd_ref, sd_ref = (next(it) for _ in range(4))
    s0_ref = next(it) if has_init else None
    y_ref, so_ref, st_ref = next(it), next(it), next(it)

    if has_init:
        st_ref[...] = s0_ref[0, 0]
    else:
        st_ref[...] = jnp.zeros_like(st_ref)
    decay = dec_ref[0]
    q_decay, k_decay, s_decay = qd_ref[0], kd_ref[0], sd_ref[0]
    for c in range(n_blocks):
        rows = slice(c * blk, (c + 1) * blk)
        q, k, v = q_ref[rows, :], k_ref[rows, :], v_ref[rows, :]
        state = st_ref[...]
        scores = (_dot_nt(q, k) * decay).astype(BF16)
        o = _dot(scores, v) + _dot(q, state.astype(BF16)) * q_decay
        k_dec = (k.astype(F32) * k_decay).astype(BF16)
        st_ref[...] = s_decay * state + _dot_tn(k_dec, v)
        mu = jnp.mean(o, axis=-1, keepdims=True)
        cen = o - mu
        var = jnp.mean(cen * cen, axis=-1, keepdims=True)
        y = g_ref[rows, :].astype(F32) * (cen * lax.rsqrt(var + GN_EPS) * gn_ref[...])
        y_ref[rows, :] = y.astype(BF16)
    so_ref[0, 0] = st_ref[...]


def _retention_decays(heads, blk):
    log_gamma = np.log1p(-np.exp2(-5.0 - np.arange(heads, dtype=np.float64)))
    idx = np.arange(blk, dtype=np.float64)
    diff = idx[:, None] - idx[None, :]
    decay = np.where(diff >= 0, np.exp(np.maximum(diff, 0.0)[None] * log_gamma[:, None, None]), 0.0)
    q_decay = np.exp((idx + 1.0)[None, :] * log_gamma[:, None])[..., None]
    k_decay = np.exp((blk - 1.0 - idx)[None, :] * log_gamma[:, None])[..., None]
    s_decay = np.exp(blk * log_gamma)[:, None, None]
    return tuple(jnp.asarray(a, F32) for a in (decay, q_decay, k_decay, s_decay))


def _retention(q, k, v, gate, gn_g, state0, *, batch, heads):
    t, d = q.shape
    s_len = t // batch
    key_dim = d // heads
    val_dim = v.shape[1] // heads
    blk = min(RET_BLOCK, s_len)
    decay, q_decay, k_decay, s_decay = _retention_decays(heads, blk)
    qk_blk = pl.BlockSpec((s_len, key_dim), lambda b, h: (b, h))
    v_blk = pl.BlockSpec((s_len, val_dim), lambda b, h: (b, h))
    per_head = lambda shape: pl.BlockSpec((1,) + shape, lambda b, h: (h, 0, 0))
    state_blk = pl.BlockSpec((1, 1, key_dim, val_dim), lambda b, h: (b, h, 0, 0))
    in_specs = [qk_blk, qk_blk, v_blk, v_blk, pl.BlockSpec((1, val_dim), lambda b, h: (0, h)),
                per_head((blk, blk)), per_head((blk, 1)), per_head((blk, 1)), per_head((1, 1))]
    args = [q, k, v, gate, gn_g.reshape(1, -1).astype(F32), decay, q_decay, k_decay, s_decay]
    if state0 is not None:
        in_specs.append(state_blk)
        args.append(state0)
    kern = functools.partial(_retention_kernel, n_blocks=s_len // blk, blk=blk,
                             has_init=state0 is not None)
    return pl.pallas_call(
        kern, grid=(batch, heads), in_specs=in_specs, out_specs=[v_blk, state_blk],
        out_shape=[jax.ShapeDtypeStruct(v.shape, BF16),
                   jax.ShapeDtypeStruct((batch, heads, key_dim, val_dim), F32)],
        scratch_shapes=[pltpu.VMEM((key_dim, val_dim), F32)],
        compiler_params=_params(2))(*args)


def _rotary_tables(pos, half):
    inv_freq = ROPE_BASE ** (-jnp.arange(half, dtype=F32) / half)
    ang = pos.astype(F32)[:, None] * inv_freq[None, :]
    return jnp.cos(ang), jnp.sin(ang)


TM_PROMPT = 512
TM_FFN = 1024


def kernel(x_prompt, x_sample, cache_chunk_k, cache_chunk_v, cache_fox_k, cache_fox_v, cache_fox_logf,
           state_ret, norm_g, w_ffn_in, w_ffn_out, a_w_in, a_g_q, a_g_k, a_rel_table, a_w_out,
           b_w_in, b_bias_f, b_g_q, b_g_k, b_w_out, c_w_in, c_gn_g, c_w_out):
    batch, seq, d = x_prompt.shape
    dec_batch, dec_seq, _ = x_sample.shape
    past = cache_fox_k.shape[2]
    depth = norm_g.shape[0]
    xp = x_prompt.reshape(batch * seq, d)
    xs = x_sample.reshape(dec_batch * dec_seq, d)
    head_dim = d // A_HEADS
    outs = {name: [] for name in ("a_kp", "a_vp", "a_ks", "a_vs", "b_kp", "b_vp", "b_fp",
                                  "b_ks", "b_vs", "b_fs", "c_sp", "c_ss")}

    for i in range(depth):
        g = norm_g[i].astype(F32)
        kind, j = i % N_MIXERS, i // N_MIXERS
        w1_in, w1_out = w_ffn_in[i, 0].astype(BF16), w_ffn_out[i, 0].astype(BF16)
        w2_in, w2_out = w_ffn_in[i, 1].astype(BF16), w_ffn_out[i, 1].astype(BF16)
        gains1 = jnp.stack([g[0], g[0], g[1]])
        gains2 = jnp.stack([g[2], g[3], g[3]])
        xp, hp = _ffn(xp, gains1, w1_in, w1_out, has_post=False, has_hout=True, tm=TM_FFN)
        xs, hs = _ffn(xs, gains1, w1_in, w1_out, has_post=False, has_hout=True, tm=TM_FFN)

        if kind == 0:
            keep = min(A_WINDOW, seq)
            qp, kbp, vbp, kp, vp = _qkv_proj(hp, a_w_in[j], a_g_q[j], a_g_k[j], heads=A_HEADS,
                                             seq=seq, keep=keep, tm=TM_PROMPT)
            qs, kbs, vbs, ks, vs = _qkv_proj(hs, a_w_in[j], a_g_q[j], a_g_k[j], heads=A_HEADS,
                                             tm=TM_PROMPT)
            op = _band_prompt(qp, kbp, vbp, a_rel_table[j], batch=batch, tq=256)
            os_ = _band_sample(qs, kbs, vbs, cache_chunk_k, cache_chunk_v, a_rel_table[j],
                               layer=j, batch=dec_batch)
            outs["a_kp"].append(_token_major_heads(kp, A_HEADS))
            outs["a_vp"].append(_token_major_heads(vp, A_HEADS))
            outs["a_ks"].append(ks.reshape(dec_batch, dec_seq, A_HEADS, head_dim))
            outs["a_vs"].append(vs.reshape(dec_batch, dec_seq, A_HEADS, head_dim))
            w_o = a_w_out[j].astype(BF16)
        elif kind == 1:
            forget = (b_w_in[j][:, 3 * d:], b_bias_f[j])
            qp, kbp, vbp, kp, vp, fpt = _qkv_proj(hp, b_w_in[j], b_g_q[j], b_g_k[j], heads=B_HEADS,
                                                  seq=seq, forget=forget, tm=TM_PROMPT)
            qs, kbs, vbs, ks, vs, fs = _qkv_proj(hs, b_w_in[j], b_g_q[j], b_g_k[j], heads=B_HEADS,
                                                 forget=forget, tm=TM_PROMPT)
            op = _fox_prompt(qp, kbp, vbp, fpt, batch=batch, tq=256)
            total = past + dec_seq
            padded = -(-total // LANES) * LANES
            lf_all = jnp.concatenate([cache_fox_logf[j].astype(F32),
                                      fs.reshape(dec_batch, dec_seq, B_HEADS)], axis=1)
            lf_all = jnp.pad(jnp.swapaxes(lf_all, 1, 2), ((0, 0), (0, 0), (0, padded - total)))
            os_ = _fox_sample(qs, kbs, vbs, cache_fox_k[j], cache_fox_v[j], _logf_cumsum(lf_all),
                              batch=dec_batch)
            outs["b_kp"].append(_token_major_heads(kp, B_HEADS))
            outs["b_vp"].append(_token_major_heads(vp, B_HEADS))
            outs["b_fp"].append(jnp.swapaxes(fpt, 1, 2))
            outs["b_ks"].append(ks.reshape(dec_batch, dec_seq, B_HEADS, head_dim))
            outs["b_vs"].append(vs.reshape(dec_batch, dec_seq, B_HEADS, head_dim))
            outs["b_fs"].append(fs.reshape(dec_batch, dec_seq, B_HEADS))
            w_o = b_w_out[j].astype(BF16)
        else:
            half = d // C_HEADS // 2
            cos_p, sin_p = _rotary_tables(jnp.arange(seq), half)
            cos_s, sin_s = _rotary_tables(past + jnp.arange(dec_seq), half)
            cos_s, sin_s = jnp.tile(cos_s, (dec_batch, 1)), jnp.tile(sin_s, (dec_batch, 1))
            qp, kp, vp, gp = _ret_proj(hp, c_w_in[j], cos_p, sin_p, heads=C_HEADS, tm=TM_PROMPT)
            qs, ks, vs, gs = _ret_proj(hs, c_w_in[j], cos_s, sin_s, heads=C_HEADS, tm=TM_PROMPT)
            op, sp = _retention(qp, kp, vp, gp, c_gn_g[j], None, batch=batch, heads=C_HEADS)
            os_, ss = _retention(qs, ks, vs, gs, c_gn_g[j], state_ret[j].astype(F32),
                                 batch=dec_batch, heads=C_HEADS)
            outs["c_sp"].append(sp)
            outs["c_ss"].append(ss)
            w_o = c_w_out[j].astype(BF16)

        xp = _ffn(xp, gains2, w2_in, w2_out, mix=(op, w_o), has_post=True, has_hout=False,
                  tm=TM_FFN)
        xs = _ffn(xs, gains2, w2_in, w2_out, mix=(os_, w_o), has_post=True, has_hout=False,
                  tm=TM_FFN)

    stack = lambda name: jnp.stack(outs[name])
    return (xp.reshape(batch, seq, d), xs.reshape(dec_batch, dec_seq, d),
            stack("a_kp"), stack("a_vp"), stack("a_ks"), stack("a_vs"),
            stack("b_kp"), stack("b_vp"), stack("b_fp"), stack("b_ks"), stack("b_vs"), stack("b_fs"),
            stack("c_sp"), stack("c_ss"))
```

```python
import functools

import numpy as np
import jax
import jax.numpy as jnp
from jax import lax
from jax.experimental import pallas as pl
from jax.experimental.pallas import tpu as pltpu

CHUNK = 64
A_HEADS = 16
A_LEFT_CHUNKS = 8
A_WINDOW = A_LEFT_CHUNKS * CHUNK
A_BAND = A_WINDOW + CHUNK
A_REL_CLIP = 128
B_HEADS = 16
C_HEADS = 4
ROPE_BASE = 10000.0
NORM_EPS = 1e-6
GN_EPS = 1e-5
NEG_INF = -1e30
N_MIXERS = 3
LOG2E = 1.4426950408889634

LANES = 128
HEAD_PAIR = LANES
MXU_DIM = 256
VMEM_LIMIT = 56 * 1024 * 1024

BF16 = jnp.bfloat16
F32 = jnp.float32


def _params(n_axes, vmem=VMEM_LIMIT):
    return pltpu.CompilerParams(dimension_semantics=("arbitrary",) * n_axes,
                                vmem_limit_bytes=vmem)


def _resident(shape):
    nd = len(shape)
    return pl.BlockSpec(shape, lambda *_: (0,) * nd, pipeline_mode=pl.Buffered(1))


def _dot(a, b):
    return jnp.dot(a, b, preferred_element_type=F32)


def _dot_nt(a, b):
    return lax.dot_general(a, b, (((1,), (1,)), ((), ())), preferred_element_type=F32)


def _dot_tn(a, b):
    return lax.dot_general(a, b, (((0,), (0,)), ((), ())), preferred_element_type=F32)


def _rms(x, g):
    return x * lax.rsqrt(jnp.mean(x * x, axis=-1, keepdims=True) + NORM_EPS) * g


def _ffn_kernel(*refs, has_mix, has_post, has_hout, bounds, d_ff, sub_rows):
    it = iter(refs)
    x_ref = next(it)
    if has_mix:
        o_ref, wo_ref = next(it), next(it)
    g_ref, win_ref, wout_ref = next(it), next(it), next(it)
    xo_ref = next(it)
    ho_ref = next(it) if has_hout else None

    starts = list(range(0, x_ref.shape[0], sub_rows))

    def prologue(r):
        rows = slice(r, r + sub_rows)
        x = x_ref[rows, :]
        if has_mix:
            x = x + _dot(o_ref[rows, :], wo_ref[...])
        return x, _rms(x, g_ref[0:1, :]).astype(BF16)

    nxt = prologue(starts[0])
    for n, r in enumerate(starts):
        rows = slice(r, r + sub_rows)
        x, h = nxt
        if n + 1 < len(starts):
            nxt = prologue(starts[n + 1])
        y = None
        for a, b in bounds:
            gate = _dot(h, win_ref[:, a:b])
            up = _dot(h, win_ref[:, d_ff + a:d_ff + b])
            act = (gate * jax.nn.sigmoid(gate) * up).astype(BF16)
            part = _dot(act, wout_ref[a:b, :])
            y = part if y is None else y + part
        x = x + 0.5 * y
        if has_post:
            x = _rms(x, g_ref[1:2, :])
        xo_ref[rows, :] = x
        if has_hout:
            ho_ref[rows, :] = _rms(x, g_ref[2:3, :]).astype(BF16)


FFN_SUB_ROWS = MXU_DIM


def _ffn_bounds(d_ff):
    step = 4 * MXU_DIM
    return tuple((a, min(a + step, d_ff)) for a in range(0, d_ff, step))


def _ffn(x, gains, w_in, w_out, *, mix=None, has_post, has_hout, tm):
    t, d = x.shape
    d_ff = w_out.shape[0]
    tm = min(tm, t)
    row = lambda i: (i, 0)
    in_specs = [pl.BlockSpec((tm, d), row)]
    args = [x]
    if mix is not None:
        o, w_o = mix
        in_specs += [pl.BlockSpec((tm, o.shape[1]), row), _resident(w_o.shape)]
        args += [o, w_o]
    in_specs += [_resident(gains.shape), _resident(w_in.shape), _resident(w_out.shape)]
    args += [gains, w_in, w_out]
    out_shape = [jax.ShapeDtypeStruct((t, d), F32)]
    out_specs = [pl.BlockSpec((tm, d), row)]
    if has_hout:
        out_shape.append(jax.ShapeDtypeStruct((t, d), BF16))
        out_specs.append(pl.BlockSpec((tm, d), row))
    kern = functools.partial(_ffn_kernel, has_mix=mix is not None, has_post=has_post,
                             has_hout=has_hout, bounds=_ffn_bounds(d_ff), d_ff=d_ff,
                             sub_rows=min(FFN_SUB_ROWS, tm))
    res = pl.pallas_call(kern, grid=(t // tm,), in_specs=in_specs, out_specs=out_specs,
                         out_shape=out_shape, compiler_params=_params(1))(*args)
    return res if has_hout else res[0]


def _head_mean_square(x, e_ref, head_dim):
    sq = (x * x).astype(BF16)
    parts = [_dot(sq[:, c:c + MXU_DIM], e_ref[...]) for c in range(0, x.shape[1], MXU_DIM)]
    return jnp.concatenate(parts, axis=1) * (1.0 / head_dim)


def _log_sigmoid(x):
    return jnp.minimum(x, 0.0) - jnp.log1p(jnp.exp(-jnp.abs(x)))


PROJ_SUB_ROWS = MXU_DIM


def _qkv_kernel(*refs, d, head_dim, has_f, feature_major, n_prev, tail_tiles, sub_rows):
    it = iter(refs)
    h_ref, wq_ref, wk_ref, wv_ref, e_ref, gq_ref, gk_ref = (next(it) for _ in range(7))
    if has_f:
        wf_ref, bf_ref = next(it), next(it)
    if n_prev:
        pk_ref, pv_ref = next(it), next(it)
    q_ref, kb_ref, vb_ref, kf_ref, vf_ref = (next(it) for _ in range(5))
    lf_ref = next(it) if has_f else None
    heads = d // head_dim

    kept_f32 = []
    for r0 in range(0, h_ref.shape[0], sub_rows):
        rows = slice(r0, r0 + sub_rows)
        h = h_ref[rows, :]
        if feature_major:
            k = _dot_nt(wk_ref[...], h).reshape(heads, head_dim, sub_rows)
            ms = jnp.mean(k * k, axis=1, keepdims=True)
            k = (k * lax.rsqrt(ms + NORM_EPS)).reshape(d, sub_rows) * gk_ref[...]
            v = _dot_nt(wv_ref[...], h)
            out = (0, slice(None), rows)
        else:
            k = _dot(h, wk_ref[...])
            k = k * lax.rsqrt(_head_mean_square(k, e_ref, head_dim) + NORM_EPS) * gk_ref[...]
            v = _dot(h, wv_ref[...])
            out = (rows, slice(None))
        kb_ref[out] = k.astype(BF16)
        vb_ref[out] = v.astype(BF16)
        f32_out = (n_prev,) + out if feature_major else out
        if tail_tiles is None:
            kf_ref[f32_out] = k
            vf_ref[f32_out] = v
        else:
            kept_f32.append((f32_out, k, v))
        q = _dot(h, wq_ref[...])
        q = q * lax.rsqrt(_head_mean_square(q, e_ref, head_dim) + NORM_EPS) * gq_ref[...]
        q_ref[rows, :] = q.astype(BF16)
        if has_f:
            if feature_major:
                lf_ref[out] = _log_sigmoid(_dot_nt(wf_ref[...], h) + bf_ref[...])
            else:
                lf_ref[out] = _log_sigmoid(_dot(h, wf_ref[...]) + bf_ref[...])

    def write_kept():
        for f32_out, k, v in kept_f32:
            kf_ref[f32_out] = k
            vf_ref[f32_out] = v
        if n_prev:
            kf_ref[0:n_prev] = pk_ref[...]
            vf_ref[0:n_prev] = pv_ref[...]

    if tail_tiles is None:
        write_kept()
    else:
        per_seq, kept = tail_tiles
        pl.when(pl.program_id(0) % per_seq >= per_seq - kept)(write_kept)


def _qkv_proj(h, w, g_q, g_k, *, heads, seq=None, keep=None, prev=None, forget=None, tm):
    t, d = h.shape
    tm = min(tm, t)
    head_dim = d // heads
    scale = head_dim ** -0.5 * LOG2E
    blk = np.kron(np.eye(MXU_DIM // head_dim), np.ones((head_dim, head_dim)))
    e = jnp.asarray(blk, BF16)
    gq = (jnp.tile(g_q.astype(F32), heads) * scale).reshape(1, d)
    gk = jnp.tile(g_k.astype(F32), heads)
    wq, wk, wv = (w[:, c * d:(c + 1) * d].astype(BF16) for c in range(3))
    row = lambda i: (i, 0)
    tail_tiles, n_prev = None, 0
    if seq is None:
        gk = gk.reshape(1, d)
        kv_shape, kv_spec = (t, d), pl.BlockSpec((tm, d), row)
        f32_shape, f32_spec = kv_shape, kv_spec
    else:
        keep = seq if keep is None else keep
        tm = min(tm, keep)
        assert seq % tm == 0 and keep % tm == 0
        per_seq, kept = seq // tm, keep // tm
        gk, wk, wv = gk.reshape(d, 1), wk.T, wv.T
        kv_shape = (t // seq, d, seq)
        kv_spec = pl.BlockSpec((1, d, tm), lambda i: (i // per_seq, 0, i % per_seq))
        if kept != per_seq:
            tail_tiles = (per_seq, kept)
        n_prev = 0 if prev is None else prev[0].shape[0]
        f32_shape = (n_prev + 1, t // seq, d, keep)
        f32_index = lambda i: (0, i // per_seq, 0, jnp.maximum(i % per_seq - (per_seq - kept), 0))
        f32_spec = pl.BlockSpec((n_prev + 1, 1, d, tm), f32_index)
    in_specs = [pl.BlockSpec((tm, d), row)] + [_resident((d, d))] * 3 + [
        _resident(e.shape), _resident(gq.shape), _resident(gk.shape)]
    args = [h, wq, wk, wv, e, gq, gk]
    out_shape = ([jax.ShapeDtypeStruct((t, d), BF16)] + [jax.ShapeDtypeStruct(kv_shape, BF16)] * 2
                 + [jax.ShapeDtypeStruct(f32_shape, F32)] * 2)
    out_specs = [pl.BlockSpec((tm, d), row), kv_spec, kv_spec, f32_spec, f32_spec]
    if forget is not None:
        w_f, bias_f = forget
        if seq is None:
            w_f, bias_f = w_f.astype(BF16), bias_f.reshape(1, heads).astype(F32)
            out_shape.append(jax.ShapeDtypeStruct((t, heads), F32))
            out_specs.append(pl.BlockSpec((tm, heads), row))
        else:
            w_f, bias_f = w_f.T.astype(BF16), bias_f.reshape(heads, 1).astype(F32)
            out_shape.append(jax.ShapeDtypeStruct((t // seq, heads, seq), F32))
            out_specs.append(pl.BlockSpec((1, heads, tm), kv_spec.index_map))
        args += [w_f, bias_f]
        in_specs += [_resident(w_f.shape), _resident(bias_f.shape)]
    if n_prev:
        args += list(prev)
        in_specs += [pl.BlockSpec((n_prev, 1, d, tm), f32_index)] * 2
    kern = functools.partial(_qkv_kernel, d=d, head_dim=head_dim, has_f=forget is not None,
                             feature_major=seq is not None, n_prev=n_prev, tail_tiles=tail_tiles,
                             sub_rows=min(PROJ_SUB_ROWS, tm))
    return pl.pallas_call(kern, grid=(t // tm,), in_specs=in_specs, out_specs=out_specs,
                          out_shape=out_shape, compiler_params=_params(1))(*args)


def _head_masks(shape):
    lane = lax.broadcasted_iota(jnp.int32, shape, len(shape) - 1)
    first = lane < (HEAD_PAIR // 2)
    return first, jnp.logical_not(first)


def _softmax_pv(score_parts, value_parts):
    m = None
    for s in score_parts:
        mx = jnp.max(s, axis=-1, keepdims=True)
        m = mx if m is None else jnp.maximum(m, mx)
    acc, l = None, None
    for s, v in zip(score_parts, value_parts):
        p = jnp.exp2(s - m)
        ls = jnp.sum(p, axis=-1, keepdims=True)
        pv = _dot(p.astype(BF16), v)
        acc = pv if acc is None else acc + pv
        l = ls if l is None else l + ls
    return acc * (1.0 / l)


def _stack_heads(q, first, second):
    zero = jnp.zeros_like(q)
    return jnp.concatenate([jnp.where(first, q, zero), jnp.where(second, q, zero)], axis=0)


def _pv_normalised(p, vt, first, tq):
    ones = jnp.ones(vt.shape, vt.dtype)
    o2 = _dot_nt(p, jnp.concatenate([vt, ones], axis=0))
    o2 = o2[:, :HEAD_PAIR] * (1.0 / o2[:, HEAD_PAIR:])
    return jnp.where(first, o2[:tq], o2[tq:])


def _toeplitz_bias(rel_table, n_rows, n_cols, offset):
    n_diag = n_rows + n_cols - 1
    u = np.arange(n_diag) - (n_rows - 1)
    idx = np.clip(offset - u, -A_REL_CLIP, A_REL_CLIP) + A_REL_CLIP
    diag = jnp.pad(rel_table.astype(F32)[:, idx], ((0, 0), (0, 1)))
    heads = diag.shape[0]
    skew = jnp.tile(diag, (1, n_rows))[:, :n_rows * n_diag].reshape(heads, n_rows, n_diag)
    return skew[:, :, n_rows - 1:n_rows - 1 + n_cols]


def _band_prompt_kernel(q_ref, k_ref, v_ref, bm_ref, o_ref, *, tq, window):
    s_len = q_ref.shape[0]
    first, second = _head_masks((tq, HEAD_PAIR))
    width = window + tq
    for t in range(s_len // tq):
        lo, hi = max(0, t * tq - window), (t + 1) * tq
        qs = _stack_heads(q_ref[t * tq:hi, :], first, second)
        s = _dot(qs, k_ref[0, :, lo:hi]) + bm_ref[0, :, width - (hi - lo):]
        p = jnp.exp2(s - jnp.max(s, axis=-1, keepdims=True)).astype(BF16)
        o_ref[t * tq:hi, :] = _pv_normalised(p, v_ref[0, :, lo:hi], first, tq).astype(BF16)


def _band_bias_mask(rel_table, tq):
    width = A_WINDOW + tq
    chunk_bias = _toeplitz_bias(rel_table, CHUNK, A_BAND, A_BAND - CHUNK) * LOG2E
    rows = [jnp.pad(chunk_bias, ((0, 0), (0, 0), (c * CHUNK, width - A_BAND - c * CHUNK)),
                    constant_values=NEG_INF) for c in range(tq // CHUNK)]
    bm = jnp.concatenate(rows, axis=1)
    return bm.reshape(bm.shape[0] // 2, 2 * tq, width)


def _band_prompt(q, k, v, rel_table, *, batch, tq):
    t, d = q.shape
    s_len = t // batch
    tq = min(tq, s_len)
    bm = _band_bias_mask(rel_table, tq)
    blk = pl.BlockSpec((s_len, HEAD_PAIR), lambda b, hp: (b, hp))
    kv_blk = pl.BlockSpec((1, HEAD_PAIR, s_len), lambda b, hp: (b, hp, 0))
    kern = functools.partial(_band_prompt_kernel, tq=tq, window=A_WINDOW)
    return pl.pallas_call(
        kern, grid=(batch, d // HEAD_PAIR),
        in_specs=[blk, kv_blk, kv_blk, pl.BlockSpec((1,) + bm.shape[1:], lambda b, hp: (hp, 0, 0))],
        out_specs=blk, out_shape=jax.ShapeDtypeStruct((t, d), BF16),
        compiler_params=_params(2))(q, k, v, bm)


def _rows(parts):
    return jnp.concatenate(parts, axis=0)


def _feature_major_cache(cache):
    *lead, n, heads, head_dim = cache.shape
    nl = len(lead)
    perm = tuple(range(nl)) + (nl + 1, nl + 2, nl)
    return jnp.transpose(cache, perm).reshape(*lead, heads * head_dim, n)


def _token_major_heads(x_t, heads):
    *lead, d, n = x_t.shape
    nl = len(lead)
    perm = tuple(range(nl)) + (nl + 2, nl, nl + 1)
    return jnp.transpose(x_t.reshape(*lead, heads, d // heads, n), perm)


def _band_sample_kernel(q_ref, k_ref, v_ref, ck_ref, cv_ref, bias_ref, o_ref, *, heads):
    n_new, d = q_ref.shape
    head_dim = d // heads
    win = ck_ref.shape[-1]
    cols = [slice(hd * head_dim, (hd + 1) * head_dim) for hd in range(heads)]
    s_c = _rows([_dot(q_ref[:, c], ck_ref[0, 0, c, :].astype(BF16)) for c in cols])
    s_n = _rows([_dot_nt(q_ref[:, c], k_ref[:, c]) for c in cols])
    s_c = s_c + bias_ref[:, :win]
    s_n = s_n + bias_ref[:, win:]
    m = jnp.maximum(jnp.max(s_c, axis=-1, keepdims=True), jnp.max(s_n, axis=-1, keepdims=True))
    p_c, p_n = jnp.exp2(s_c - m), jnp.exp2(s_n - m)
    inv_l = 1.0 / (jnp.sum(p_c, axis=-1, keepdims=True) + jnp.sum(p_n, axis=-1, keepdims=True))
    p_c, p_n = p_c.astype(BF16), p_n.astype(BF16)
    for hd, c in enumerate(cols):
        rows = slice(hd * n_new, (hd + 1) * n_new)
        o = _dot_nt(p_c[rows], cv_ref[0, 0, c, :].astype(BF16)) + _dot(p_n[rows], v_ref[:, c])
        o_ref[:, c] = (o * inv_l[rows]).astype(BF16)


def _band_sample(q, k, v, cache_k, cache_v, rel_table, *, layer, batch):
    t, d = q.shape
    n_new = t // batch
    _, _, win, heads, _ = cache_k.shape
    bias = (_toeplitz_bias(rel_table, n_new, win + n_new, win) * LOG2E).reshape(heads * n_new, win + n_new)
    row = pl.BlockSpec((n_new, d), lambda b: (b, 0))
    cache = pl.BlockSpec((1, 1, d, win), lambda b: (layer, b, 0, 0))
    kern = functools.partial(_band_sample_kernel, heads=heads)
    return pl.pallas_call(
        kern, grid=(batch,),
        in_specs=[row, row, row, cache, cache, _resident(bias.shape)],
        out_specs=row, out_shape=jax.ShapeDtypeStruct((t, d), BF16),
        compiler_params=_params(1))(q, k, v, _feature_major_cache(cache_k),
                                    _feature_major_cache(cache_v), bias)


def _lane_cumsum(x):
    n = x.shape[-1]
    lane = lax.broadcasted_iota(jnp.int32, x.shape, x.ndim - 1)
    shift = 1
    while shift < n:
        x = x + jnp.where(lane >= shift, pltpu.roll(x, shift, x.ndim - 1), 0.0)
        shift *= 2
    return x


def _fox_prompt_kernel(q_ref, k_ref, v_ref, lft_ref, o_ref, *, tq):
    s_len = q_ref.shape[0]
    cum = _lane_cumsum(lft_ref[0, 0]) * LOG2E
    first, second = _head_masks((tq, HEAD_PAIR))
    r = lax.broadcasted_iota(jnp.int32, (2 * tq, tq), 0)
    c = lax.broadcasted_iota(jnp.int32, (2 * tq, tq), 1)
    causal = c <= jnp.where(r >= tq, r - tq, r)
    n_tiles = s_len // tq

    def scores(t):
        qs = _stack_heads(q_ref[t * tq:(t + 1) * tq, :], first, second)
        return _dot(qs, k_ref[0, :, 0:(t + 1) * tq])

    s_next = scores(0)
    for t in range(n_tiles):
        lo, hi = t * tq, (t + 1) * tq
        s = s_next
        if t + 1 < n_tiles:
            s_next = scores(t + 1)
        s = jnp.concatenate([s[:tq] - cum[0:1, 0:hi], s[tq:] - cum[1:2, 0:hi]], axis=0)
        parts = [jnp.where(causal, s[:, lo:hi], NEG_INF)]
        m = jnp.max(parts[0], axis=-1, keepdims=True)
        if t > 0:
            parts.insert(0, s[:, 0:lo])
            m = jnp.maximum(m, jnp.max(parts[0], axis=-1, keepdims=True))
        p = jnp.concatenate([jnp.exp2(x - m) for x in parts], axis=1).astype(BF16)
        o_ref[lo:hi, :] = _pv_normalised(p, v_ref[0, :, 0:hi], first, tq).astype(BF16)


def _fox_prompt(q, k, v, logf_t, *, batch, tq):
    t, d = q.shape
    s_len = t // batch
    tq = min(tq, s_len)
    heads = logf_t.shape[1]
    blk = pl.BlockSpec((s_len, HEAD_PAIR), lambda b, hp: (b, hp))
    kv_blk = pl.BlockSpec((1, HEAD_PAIR, s_len), lambda b, hp: (b, hp, 0))
    kern = functools.partial(_fox_prompt_kernel, tq=tq)
    return pl.pallas_call(
        kern, grid=(batch, d // HEAD_PAIR),
        in_specs=[blk, kv_blk, kv_blk, pl.BlockSpec((1, 1, 2, s_len), lambda b, hp: (b, hp, 0, 0))],
        out_specs=blk, out_shape=jax.ShapeDtypeStruct((t, d), BF16),
        compiler_params=_params(2))(q, k, v, logf_t.reshape(batch, heads // 2, 2, s_len))


def _logf_cumsum_kernel(lf_ref, cum_ref):
    cum_ref[0] = _lane_cumsum(lf_ref[0]) * LOG2E


def _logf_cumsum(logf_t):
    blk = pl.BlockSpec((1,) + logf_t.shape[1:], lambda b: (b, 0, 0))
    return pl.pallas_call(_logf_cumsum_kernel, grid=(logf_t.shape[0],), in_specs=[blk], out_specs=blk,
                          out_shape=jax.ShapeDtypeStruct(logf_t.shape, F32),
                          compiler_params=_params(1))(logf_t)


FOX_SAMPLE_KEYS = 1024


def _fox_sample_kernel(q_ref, k_ref, v_ref, ck_ref, cv_ref, cc_ref, cn_ref, o_ref,
                       m_ref, l_ref, acc_ref, *, heads):
    n_new, d = q_ref.shape
    head_dim = d // heads
    kt = pl.program_id(1)
    cols = [slice(hd * head_dim, (hd + 1) * head_dim) for hd in range(heads)]
    rows = [slice(hd * n_new, (hd + 1) * n_new) for hd in range(heads)]

    @pl.when(kt == 0)
    def _():
        m_ref[...] = jnp.full(m_ref.shape, NEG_INF, F32)
        l_ref[...] = jnp.zeros(l_ref.shape, F32)
        acc_ref[...] = jnp.zeros(acc_ref.shape, F32)

    def update(s, head_pv):
        m_old = m_ref[...]
        m_new = jnp.maximum(m_old, jnp.max(s, axis=-1, keepdims=True))
        alpha = jnp.exp2(m_old - m_new)
        p = jnp.exp2(s - m_new)
        l_ref[...] = alpha * l_ref[...] + jnp.sum(p, axis=-1, keepdims=True)
        p = p.astype(BF16)
        acc_ref[...] = alpha * acc_ref[...] + _rows([head_pv(hd, p[r]) for hd, r in enumerate(rows)])
        m_ref[...] = m_new

    update(_rows([_dot(q_ref[:, c], ck_ref[0, c, :].astype(BF16)) - cc_ref[0, hd:hd + 1, :]
                  for hd, c in enumerate(cols)]),
           lambda hd, p: _dot_nt(p, cv_ref[0, cols[hd], :].astype(BF16)))

    @pl.when(kt == pl.num_programs(1) - 1)
    def _():
        causal = (lax.broadcasted_iota(jnp.int32, (n_new, n_new), 1)
                  <= lax.broadcasted_iota(jnp.int32, (n_new, n_new), 0))
        s = _rows([jnp.where(causal, _dot_nt(q_ref[:, c], k_ref[:, c]) - cn_ref[0, hd:hd + 1, 0:n_new],
                             NEG_INF) for hd, c in enumerate(cols)])
        update(s, lambda hd, p: _dot(p, v_ref[:, cols[hd]]))
        o = acc_ref[...] * (1.0 / l_ref[...])
        for hd, c in enumerate(cols):
            o_ref[:, c] = o[rows[hd]].astype(BF16)


def _fox_sample(q, k, v, cache_k, cache_v, cum, *, batch):
    t, d = q.shape
    n_new = t // batch
    _, past, heads, head_dim = cache_k.shape
    tk = min(FOX_SAMPLE_KEYS, past)
    assert past % tk == 0 and past % LANES == 0 and tk % LANES == 0
    row = pl.BlockSpec((n_new, d), lambda b, kt: (b, 0))
    cache = pl.BlockSpec((1, d, tk), lambda b, kt: (b, 0, kt))
    kern = functools.partial(_fox_sample_kernel, heads=heads)
    return pl.pallas_call(
        kern, grid=(batch, past // tk),
        in_specs=[row, row, row, cache, cache,
                  pl.BlockSpec((1, heads, tk), lambda b, kt: (b, 0, kt)),
                  pl.BlockSpec((1, heads, LANES), lambda b, kt: (b, 0, past // LANES))],
        out_specs=row, out_shape=jax.ShapeDtypeStruct((t, d), BF16),
        scratch_shapes=[pltpu.VMEM((heads * n_new, 1), F32), pltpu.VMEM((heads * n_new, 1), F32),
                        pltpu.VMEM((heads * n_new, head_dim), F32)],
        compiler_params=_params(2))(q, k, v, _feature_major_cache(cache_k),
                                    _feature_major_cache(cache_v), cum, cum)


RET_BLOCK = 256


def _ret_proj_kernel(h_ref, w_ref, cos_ref, sin_ref, q_ref, k_ref, v_ref, g_ref, *,
                     heads, key_dim, val_width):
    h = h_ref[...]
    cos, sin = cos_ref[...], sin_ref[...]
    half = key_dim // 2
    qk_width = heads * key_dim
    k_scale = key_dim ** -0.5
    for part, (ref, scale) in enumerate(((q_ref, 1.0), (k_ref, k_scale))):
        for hd in range(heads):
            c0 = part * qk_width + hd * key_dim
            x = _dot(h, w_ref[:, c0:c0 + key_dim])
            x1, x2 = x[:, :half], x[:, half:]
            o0 = hd * key_dim
            ref[:, o0:o0 + half] = ((x1 * cos - x2 * sin) * scale).astype(BF16)
            ref[:, o0 + half:o0 + key_dim] = ((x2 * cos + x1 * sin) * scale).astype(BF16)
    v0 = 2 * qk_width
    step = 4 * MXU_DIM
    for c in range(0, val_width, step):
        v_ref[:, c:c + step] = _dot(h, w_ref[:, v0 + c:v0 + c + step]).astype(BF16)
        g = _dot(h, w_ref[:, v0 + val_width + c:v0 + val_width + c + step])
        g_ref[:, c:c + step] = (g * jax.nn.sigmoid(g)).astype(BF16)


def _ret_proj(h, w, cos, sin, *, heads, tm):
    t, d = h.shape
    tm = min(tm, t, cos.shape[0])
    key_dim = d // heads
    val_width = 2 * d
    n_period = cos.shape[0] // tm
    row = lambda i: (i, 0)
    tab = pl.BlockSpec((tm, key_dim // 2), lambda i: (i % n_period, 0))
    kern = functools.partial(_ret_proj_kernel, heads=heads, key_dim=key_dim, val_width=val_width)
    return pl.pallas_call(
        kern, grid=(t // tm,),
        in_specs=[pl.BlockSpec((tm, d), row), _resident(w.shape), tab, tab],
        out_specs=[pl.BlockSpec((tm, d), row), pl.BlockSpec((tm, d), row),
                   pl.BlockSpec((tm, val_width), row), pl.BlockSpec((tm, val_width), row)],
        out_shape=[jax.ShapeDtypeStruct((t, d), BF16), jax.ShapeDtypeStruct((t, d), BF16),
                   jax.ShapeDtypeStruct((t, val_width), BF16),
                   jax.ShapeDtypeStruct((t, val_width), BF16)],
        compiler_params=_params(1))(h, w.astype(BF16), cos, sin)


def _retention_kernel(*refs, n_blocks, blk, has_init):
    it = iter(refs)
    q_ref, k_ref, v_ref, g_ref, gn_ref = (next(it) for _ in range(5))
    dec_ref, qd_ref, kd_ref, sd_ref = (next(it) for _ in range(4))
    s0_ref = next(it) if has_init else None
    y_ref, so_ref, st_ref = next(it), next(it), next(it)

    if has_init:
        st_ref[...] = s0_ref[0, 0]
    else:
        st_ref[...] = jnp.zeros_like(st_ref)
    decay = dec_ref[0]
    q_decay, k_decay, s_decay = qd_ref[0], kd_ref[0], sd_ref[0]
    for c in range(n_blocks):
        rows = slice(c * blk, (c + 1) * blk)
        q, k, v = q_ref[rows, :], k_ref[rows, :], v_ref[rows, :]
        state = st_ref[...]
        scores = (_dot_nt(q, k) * decay).astype(BF16)
        o = _dot(scores, v) + _dot(q, state.astype(BF16)) * q_decay
        k_dec = (k.astype(F32) * k_decay).astype(BF16)
        st_ref[...] = s_decay * state + _dot_tn(k_dec, v)
        mu = jnp.mean(o, axis=-1, keepdims=True)
        cen = o - mu
        var = jnp.mean(cen * cen, axis=-1, keepdims=True)
        y = g_ref[rows, :].astype(F32) * (cen * lax.rsqrt(var + GN_EPS) * gn_ref[...])
        y_ref[rows, :] = y.astype(BF16)
    so_ref[0, 0] = st_ref[...]


def _retention_decays(heads, blk):
    log_gamma = np.log1p(-np.exp2(-5.0 - np.arange(heads, dtype=np.float64)))
    idx = np.arange(blk, dtype=np.float64)
    diff = idx[:, None] - idx[None, :]
    decay = np.where(diff >= 0, np.exp(np.maximum(diff, 0.0)[None] * log_gamma[:, None, None]), 0.0)
    q_decay = np.exp((idx + 1.0)[None, :] * log_gamma[:, None])[..., None]
    k_decay = np.exp((blk - 1.0 - idx)[None, :] * log_gamma[:, None])[..., None]
    s_decay = np.exp(blk * log_gamma)[:, None, None]
    return tuple(jnp.asarray(a, F32) for a in (decay, q_decay, k_decay, s_decay))


def _retention(q, k, v, gate, gn_g, state0, *, batch, heads):
    t, d = q.shape
    s_len = t // batch
    key_dim = d // heads
    val_dim = v.shape[1] // heads
    blk = min(RET_BLOCK, s_len)
    decay, q_decay, k_decay, s_decay = _retention_decays(heads, blk)
    qk_blk = pl.BlockSpec((s_len, key_dim), lambda b, h: (b, h))
    v_blk = pl.BlockSpec((s_len, val_dim), lambda b, h: (b, h))
    per_head = lambda shape: pl.BlockSpec((1,) + shape, lambda b, h: (h, 0, 0))
    state_blk = pl.BlockSpec((1, 1, key_dim, val_dim), lambda b, h: (b, h, 0, 0))
    in_specs = [qk_blk, qk_blk, v_blk, v_blk, pl.BlockSpec((1, val_dim), lambda b, h: (0, h)),
                per_head((blk, blk)), per_head((blk, 1)), per_head((blk, 1)), per_head((1, 1))]
    args = [q, k, v, gate, gn_g.reshape(1, -1).astype(F32), decay, q_decay, k_decay, s_decay]
    if state0 is not None:
        in_specs.append(state_blk)
        args.append(state0)
    kern = functools.partial(_retention_kernel, n_blocks=s_len // blk, blk=blk,
                             has_init=state0 is not None)
    return pl.pallas_call(
        kern, grid=(batch, heads), in_specs=in_specs, out_specs=[v_blk, state_blk],
        out_shape=[jax.ShapeDtypeStruct(v.shape, BF16),
                   jax.ShapeDtypeStruct((batch, heads, key_dim, val_dim), F32)],
        scratch_shapes=[pltpu.VMEM((key_dim, val_dim), F32)],
        compiler_params=_params(2))(*args)


def _rotary_tables(pos, half):
    inv_freq = ROPE_BASE ** (-jnp.arange(half, dtype=F32) / half)
    ang = pos.astype(F32)[:, None] * inv_freq[None, :]
    return jnp.cos(ang), jnp.sin(ang)


TM_PROMPT = 1024
TM_FFN = 1024


def kernel(x_prompt, x_sample, cache_chunk_k, cache_chunk_v, cache_fox_k, cache_fox_v, cache_fox_logf,
           state_ret, norm_g, w_ffn_in, w_ffn_out, a_w_in, a_g_q, a_g_k, a_rel_table, a_w_out,
           b_w_in, b_bias_f, b_g_q, b_g_k, b_w_out, c_w_in, c_gn_g, c_w_out):
    batch, seq, d = x_prompt.shape
    dec_batch, dec_seq, _ = x_sample.shape
    past = cache_fox_k.shape[2]
    depth = norm_g.shape[0]
    xp = x_prompt.reshape(batch * seq, d)
    xs = x_sample.reshape(dec_batch * dec_seq, d)
    head_dim = d // A_HEADS
    outs = {name: [] for name in ("a_ks", "a_vs", "b_fp", "b_ks", "b_vs", "b_fs", "c_sp", "c_ss")}
    kv_prompt = {"a": None, "b": None}

    for i in range(depth):
        g = norm_g[i].astype(F32)
        kind, j = i % N_MIXERS, i // N_MIXERS
        w1_in, w1_out = w_ffn_in[i, 0].astype(BF16), w_ffn_out[i, 0].astype(BF16)
        w2_in, w2_out = w_ffn_in[i, 1].astype(BF16), w_ffn_out[i, 1].astype(BF16)
        gains1 = jnp.stack([g[0], g[0], g[1]])
        gains2 = jnp.stack([g[2], g[3], g[3]])
        xp, hp = _ffn(xp, gains1, w1_in, w1_out, has_post=False, has_hout=True, tm=TM_FFN)
        xs, hs = _ffn(xs, gains1, w1_in, w1_out, has_post=False, has_hout=True, tm=TM_FFN)

        if kind == 0:
            keep = min(A_WINDOW, seq)
            qp, kbp, vbp, kp, vp = _qkv_proj(hp, a_w_in[j], a_g_q[j], a_g_k[j], heads=A_HEADS, seq=seq,
                                             keep=keep, prev=kv_prompt["a"], tm=min(TM_PROMPT, keep))
            kv_prompt["a"] = (kp, vp)
            qs, kbs, vbs, ks, vs = _qkv_proj(hs, a_w_in[j], a_g_q[j], a_g_k[j], heads=A_HEADS,
                                             tm=TM_PROMPT)
            op = _band_prompt(qp, kbp, vbp, a_rel_table[j], batch=batch, tq=256)
            os_ = _band_sample(qs, kbs, vbs, cache_chunk_k, cache_chunk_v, a_rel_table[j],
                               layer=j, batch=dec_batch)
            outs["a_ks"].append(ks.reshape(dec_batch, dec_seq, A_HEADS, head_dim))
            outs["a_vs"].append(vs.reshape(dec_batch, dec_seq, A_HEADS, head_dim))
            w_o = a_w_out[j].astype(BF16)
        elif kind == 1:
            forget = (b_w_in[j][:, 3 * d:], b_bias_f[j])
            qp, kbp, vbp, kp, vp, fpt = _qkv_proj(hp, b_w_in[j], b_g_q[j], b_g_k[j], heads=B_HEADS, seq=seq,
                                                  prev=kv_prompt["b"], forget=forget, tm=TM_PROMPT)
            kv_prompt["b"] = (kp, vp)
            qs, kbs, vbs, ks, vs, fs = _qkv_proj(hs, b_w_in[j], b_g_q[j], b_g_k[j], heads=B_HEADS,
                                                 forget=forget, tm=TM_PROMPT)
            op = _fox_prompt(qp, kbp, vbp, fpt, batch=batch, tq=256)
            total = past + dec_seq
            padded = -(-total // LANES) * LANES
            lf_all = jnp.concatenate([cache_fox_logf[j].astype(F32),
                                      fs.reshape(dec_batch, dec_seq, B_HEADS)], axis=1)
            lf_all = jnp.pad(jnp.swapaxes(lf_all, 1, 2), ((0, 0), (0, 0), (0, padded - total)))
            os_ = _fox_sample(qs, kbs, vbs, cache_fox_k[j], cache_fox_v[j], _logf_cumsum(lf_all),
                              batch=dec_batch)
            outs["b_fp"].append(jnp.swapaxes(fpt, 1, 2))
            outs["b_ks"].append(ks.reshape(dec_batch, dec_seq, B_HEADS, head_dim))
            outs["b_vs"].append(vs.reshape(dec_batch, dec_seq, B_HEADS, head_dim))
            outs["b_fs"].append(fs.reshape(dec_batch, dec_seq, B_HEADS))
            w_o = b_w_out[j].astype(BF16)
        else:
            half = d // C_HEADS // 2
            cos_p, sin_p = _rotary_tables(jnp.arange(seq), half)
            cos_s, sin_s = _rotary_tables(past + jnp.arange(dec_seq), half)
            cos_s, sin_s = jnp.tile(cos_s, (dec_batch, 1)), jnp.tile(sin_s, (dec_batch, 1))
            qp, kp, vp, gp = _ret_proj(hp, c_w_in[j], cos_p, sin_p, heads=C_HEADS, tm=TM_PROMPT)
            qs, ks, vs, gs = _ret_proj(hs, c_w_in[j], cos_s, sin_s, heads=C_HEADS, tm=TM_PROMPT)
            op, sp = _retention(qp, kp, vp, gp, c_gn_g[j], None, batch=batch, heads=C_HEADS)
            os_, ss = _retention(qs, ks, vs, gs, c_gn_g[j], state_ret[j].astype(F32),
                                 batch=dec_batch, heads=C_HEADS)
            outs["c_sp"].append(sp)
            outs["c_ss"].append(ss)
            w_o = c_w_out[j].astype(BF16)

        xp = _ffn(xp, gains2, w2_in, w2_out, mix=(op, w_o), has_post=True, has_hout=False,
                  tm=TM_FFN)
        xs = _ffn(xs, gains2, w2_in, w2_out, mix=(os_, w_o), has_post=True, has_hout=False,
                  tm=TM_FFN)

    stack = lambda name: jnp.stack(outs[name])
    return (xp.reshape(batch, seq, d), xs.reshape(dec_batch, dec_seq, d),
            _token_major_heads(kv_prompt["a"][0], A_HEADS), _token_major_heads(kv_prompt["a"][1], A_HEADS),
            stack("a_ks"), stack("a_vs"),
            _token_major_heads(kv_prompt["b"][0], B_HEADS), _token_major_heads(kv_prompt["b"][1], B_HEADS),
            stack("b_fp"), stack("b_ks"), stack("b_vs"), stack("b_fs"),
            stack("c_sp"), stack("c_ss"))
```

```python
import functools

import numpy as np
import jax
import jax.numpy as jnp
from jax import lax
from jax.experimental import pallas as pl
from jax.experimental.pallas import tpu as pltpu

CHUNK = 64
A_HEADS = 16
A_LEFT_CHUNKS = 8
A_WINDOW = A_LEFT_CHUNKS * CHUNK
A_BAND = A_WINDOW + CHUNK
A_REL_CLIP = 128
B_HEADS = 16
C_HEADS = 4
ROPE_BASE = 10000.0
NORM_EPS = 1e-6
GN_EPS = 1e-5
NEG_INF = -1e30
N_MIXERS = 3
LOG2E = 1.4426950408889634

LANES = 128
HEAD_PAIR = LANES
MXU_DIM = 256
VMEM_LIMIT = 56 * 1024 * 1024

BF16 = jnp.bfloat16
F32 = jnp.float32


def _params(n_axes, vmem=VMEM_LIMIT):
    return pltpu.CompilerParams(dimension_semantics=("arbitrary",) * n_axes,
                                vmem_limit_bytes=vmem)


def _resident(shape):
    nd = len(shape)
    return pl.BlockSpec(shape, lambda *_: (0,) * nd, pipeline_mode=pl.Buffered(1))


def _dot(a, b):
    return jnp.dot(a, b, preferred_element_type=F32)


def _dot_nt(a, b):
    return lax.dot_general(a, b, (((1,), (1,)), ((), ())), preferred_element_type=F32)


def _dot_tn(a, b):
    return lax.dot_general(a, b, (((0,), (0,)), ((), ())), preferred_element_type=F32)


def _rms(x, g):
    return x * lax.rsqrt(jnp.mean(x * x, axis=-1, keepdims=True) + NORM_EPS) * g


def _ffn_kernel(*refs, has_mix, has_post, has_hout, bounds, d_ff, sub_rows):
    it = iter(refs)
    x_ref = next(it)
    if has_mix:
        o_ref, wo_ref = next(it), next(it)
    g_ref, win_ref, wout_ref = next(it), next(it), next(it)
    xo_ref = next(it)
    ho_ref = next(it) if has_hout else None

    starts = list(range(0, x_ref.shape[0], sub_rows))

    def prologue(r):
        rows = slice(r, r + sub_rows)
        x = x_ref[rows, :]
        if has_mix:
            x = x + _dot(o_ref[rows, :], wo_ref[...])
        return x, _rms(x, g_ref[0:1, :]).astype(BF16)

    nxt = prologue(starts[0])
    for n, r in enumerate(starts):
        rows = slice(r, r + sub_rows)
        x, h = nxt
        if n + 1 < len(starts):
            nxt = prologue(starts[n + 1])
        y = None
        for a, b in bounds:
            gate = _dot(h, win_ref[:, a:b])
            up = _dot(h, win_ref[:, d_ff + a:d_ff + b])
            act = (gate * jax.nn.sigmoid(gate) * up).astype(BF16)
            part = _dot(act, wout_ref[a:b, :])
            y = part if y is None else y + part
        x = x + 0.5 * y
        if has_post:
            x = _rms(x, g_ref[1:2, :])
        xo_ref[rows, :] = x
        if has_hout:
            ho_ref[rows, :] = _rms(x, g_ref[2:3, :]).astype(BF16)


FFN_SUB_ROWS = MXU_DIM


def _ffn_bounds(d_ff):
    step = 4 * MXU_DIM
    return tuple((a, min(a + step, d_ff)) for a in range(0, d_ff, step))


def _ffn(x, gains, w_in, w_out, *, mix=None, has_post, has_hout, tm):
    t, d = x.shape
    d_ff = w_out.shape[0]
    tm = min(tm, t)
    row = lambda i: (i, 0)
    in_specs = [pl.BlockSpec((tm, d), row)]
    args = [x]
    if mix is not None:
        o, w_o = mix
        in_specs += [pl.BlockSpec((tm, o.shape[1]), row), _resident(w_o.shape)]
        args += [o, w_o]
    in_specs += [_resident(gains.shape), _resident(w_in.shape), _resident(w_out.shape)]
    args += [gains, w_in, w_out]
    out_shape = [jax.ShapeDtypeStruct((t, d), F32)]
    out_specs = [pl.BlockSpec((tm, d), row)]
    if has_hout:
        out_shape.append(jax.ShapeDtypeStruct((t, d), BF16))
        out_specs.append(pl.BlockSpec((tm, d), row))
    kern = functools.partial(_ffn_kernel, has_mix=mix is not None, has_post=has_post,
                             has_hout=has_hout, bounds=_ffn_bounds(d_ff), d_ff=d_ff,
                             sub_rows=min(FFN_SUB_ROWS, tm))
    res = pl.pallas_call(kern, grid=(t // tm,), in_specs=in_specs, out_specs=out_specs,
                         out_shape=out_shape, compiler_params=_params(1))(*args)
    return res if has_hout else res[0]


def _head_mean_square(x, e_ref, head_dim):
    sq = (x * x).astype(BF16)
    parts = [_dot(sq[:, c:c + MXU_DIM], e_ref[...]) for c in range(0, x.shape[1], MXU_DIM)]
    return jnp.concatenate(parts, axis=1) * (1.0 / head_dim)


def _log_sigmoid(x):
    return jnp.minimum(x, 0.0) - jnp.log1p(jnp.exp(-jnp.abs(x)))


PROJ_SUB_ROWS = MXU_DIM


def _qkv_kernel(*refs, d, head_dim, has_f, feature_major, n_prev, tail_tiles, sub_rows):
    it = iter(refs)
    h_ref, wq_ref, wk_ref, wv_ref, e_ref, gq_ref, gk_ref = (next(it) for _ in range(7))
    if has_f:
        wf_ref, bf_ref = next(it), next(it)
    if n_prev:
        pk_ref, pv_ref = next(it), next(it)
    q_ref, kb_ref, vb_ref, kf_ref, vf_ref = (next(it) for _ in range(5))
    lf_ref = next(it) if has_f else None
    heads = d // head_dim

    kept_f32 = []
    for r0 in range(0, h_ref.shape[0], sub_rows):
        rows = slice(r0, r0 + sub_rows)
        h = h_ref[rows, :]
        if feature_major:
            k = _dot_nt(wk_ref[...], h).reshape(heads, head_dim, sub_rows)
            ms = jnp.mean(k * k, axis=1, keepdims=True)
            k = (k * lax.rsqrt(ms + NORM_EPS)).reshape(d, sub_rows) * gk_ref[...]
            v = _dot_nt(wv_ref[...], h)
            out = (0, slice(None), rows)
        else:
            k = _dot(h, wk_ref[...])
            k = k * lax.rsqrt(_head_mean_square(k, e_ref, head_dim) + NORM_EPS) * gk_ref[...]
            v = _dot(h, wv_ref[...])
            out = (rows, slice(None))
        kb_ref[out] = k.astype(BF16)
        vb_ref[out] = v.astype(BF16)
        f32_out = (n_prev,) + out if feature_major else out
        if tail_tiles is None:
            kf_ref[f32_out] = k
            vf_ref[f32_out] = v
        else:
            kept_f32.append((f32_out, k, v))
        q = _dot(h, wq_ref[...])
        q = q * lax.rsqrt(_head_mean_square(q, e_ref, head_dim) + NORM_EPS) * gq_ref[...]
        q_ref[rows, :] = q.astype(BF16)
        if has_f:
            if feature_major:
                lf_ref[out] = _log_sigmoid(_dot_nt(wf_ref[...], h) + bf_ref[...])
            else:
                lf_ref[out] = _log_sigmoid(_dot(h, wf_ref[...]) + bf_ref[...])

    def write_kept():
        for f32_out, k, v in kept_f32:
            kf_ref[f32_out] = k
            vf_ref[f32_out] = v
        if n_prev:
            kf_ref[0:n_prev] = pk_ref[...]
            vf_ref[0:n_prev] = pv_ref[...]

    if tail_tiles is None:
        write_kept()
    else:
        per_seq, kept = tail_tiles
        pl.when(pl.program_id(0) % per_seq >= per_seq - kept)(write_kept)


def _qkv_proj(h, w, g_q, g_k, *, heads, seq=None, keep=None, prev=None, forget=None, tm):
    t, d = h.shape
    tm = min(tm, t)
    head_dim = d // heads
    scale = head_dim ** -0.5 * LOG2E
    blk = np.kron(np.eye(MXU_DIM // head_dim), np.ones((head_dim, head_dim)))
    e = jnp.asarray(blk, BF16)
    gq = (jnp.tile(g_q.astype(F32), heads) * scale).reshape(1, d)
    gk = jnp.tile(g_k.astype(F32), heads)
    wq, wk, wv = (w[:, c * d:(c + 1) * d].astype(BF16) for c in range(3))
    row = lambda i: (i, 0)
    tail_tiles, n_prev = None, 0
    if seq is None:
        gk = gk.reshape(1, d)
        kv_shape, kv_spec = (t, d), pl.BlockSpec((tm, d), row)
        f32_shape, f32_spec = kv_shape, kv_spec
    else:
        keep = seq if keep is None else keep
        tm = min(tm, keep)
        assert seq % tm == 0 and keep % tm == 0
        per_seq, kept = seq // tm, keep // tm
        gk, wk, wv = gk.reshape(d, 1), wk.T, wv.T
        kv_shape = (t // seq, d, seq)
        kv_spec = pl.BlockSpec((1, d, tm), lambda i: (i // per_seq, 0, i % per_seq))
        if kept != per_seq:
            tail_tiles = (per_seq, kept)
        n_prev = 0 if prev is None else prev[0].shape[0]
        f32_shape = (n_prev + 1, t // seq, d, keep)
        f32_index = lambda i: (0, i // per_seq, 0, jnp.maximum(i % per_seq - (per_seq - kept), 0))
        f32_spec = pl.BlockSpec((n_prev + 1, 1, d, tm), f32_index)
    in_specs = [pl.BlockSpec((tm, d), row)] + [_resident((d, d))] * 3 + [
        _resident(e.shape), _resident(gq.shape), _resident(gk.shape)]
    args = [h, wq, wk, wv, e, gq, gk]
    out_shape = ([jax.ShapeDtypeStruct((t, d), BF16)] + [jax.ShapeDtypeStruct(kv_shape, BF16)] * 2
                 + [jax.ShapeDtypeStruct(f32_shape, F32)] * 2)
    out_specs = [pl.BlockSpec((tm, d), row), kv_spec, kv_spec, f32_spec, f32_spec]
    if forget is not None:
        w_f, bias_f = forget
        if seq is None:
            w_f, bias_f = w_f.astype(BF16), bias_f.reshape(1, heads).astype(F32)
            out_shape.append(jax.ShapeDtypeStruct((t, heads), F32))
            out_specs.append(pl.BlockSpec((tm, heads), row))
        else:
            w_f, bias_f = w_f.T.astype(BF16), bias_f.reshape(heads, 1).astype(F32)
            out_shape.append(jax.ShapeDtypeStruct((t // seq, heads, seq), F32))
            out_specs.append(pl.BlockSpec((1, heads, tm), kv_spec.index_map))
        args += [w_f, bias_f]
        in_specs += [_resident(w_f.shape), _resident(bias_f.shape)]
    if n_prev:
        args += list(prev)
        in_specs += [pl.BlockSpec((n_prev, 1, d, tm), f32_index)] * 2
    kern = functools.partial(_qkv_kernel, d=d, head_dim=head_dim, has_f=forget is not None,
                             feature_major=seq is not None, n_prev=n_prev, tail_tiles=tail_tiles,
                             sub_rows=min(PROJ_SUB_ROWS, tm))
    return pl.pallas_call(kern, grid=(t // tm,), in_specs=in_specs, out_specs=out_specs,
                          out_shape=out_shape, compiler_params=_params(1))(*args)


def _head_masks(shape):
    lane = lax.broadcasted_iota(jnp.int32, shape, len(shape) - 1)
    first = lane < (HEAD_PAIR // 2)
    return first, jnp.logical_not(first)


def _softmax_pv(score_parts, value_parts):
    m = None
    for s in score_parts:
        mx = jnp.max(s, axis=-1, keepdims=True)
        m = mx if m is None else jnp.maximum(m, mx)
    acc, l = None, None
    for s, v in zip(score_parts, value_parts):
        p = jnp.exp2(s - m)
        ls = jnp.sum(p, axis=-1, keepdims=True)
        pv = _dot(p.astype(BF16), v)
        acc = pv if acc is None else acc + pv
        l = ls if l is None else l + ls
    return acc * (1.0 / l)


def _stack_heads(q, first, second):
    zero = jnp.zeros_like(q)
    return jnp.concatenate([jnp.where(first, q, zero), jnp.where(second, q, zero)], axis=0)


def _pv_normalised(p, vt, first, tq):
    ones = jnp.ones(vt.shape, vt.dtype)
    o2 = _dot_nt(p, jnp.concatenate([vt, ones], axis=0))
    o2 = o2[:, :HEAD_PAIR] * (1.0 / o2[:, HEAD_PAIR:])
    return jnp.where(first, o2[:tq], o2[tq:])


def _toeplitz_bias(rel_table, n_rows, n_cols, offset):
    n_diag = n_rows + n_cols - 1
    u = np.arange(n_diag) - (n_rows - 1)
    idx = np.clip(offset - u, -A_REL_CLIP, A_REL_CLIP) + A_REL_CLIP
    diag = jnp.pad(rel_table.astype(F32)[:, idx], ((0, 0), (0, 1)))
    heads = diag.shape[0]
    skew = jnp.tile(diag, (1, n_rows))[:, :n_rows * n_diag].reshape(heads, n_rows, n_diag)
    return skew[:, :, n_rows - 1:n_rows - 1 + n_cols]


def _band_prompt_kernel(q_ref, k_ref, v_ref, bm_ref, o_ref, *, tq, window):
    s_len = q_ref.shape[0]
    first, second = _head_masks((tq, HEAD_PAIR))
    width = window + tq
    n_tiles = s_len // tq

    def scores(t):
        lo, hi = max(0, t * tq - window), (t + 1) * tq
        qs = _stack_heads(q_ref[t * tq:hi, :], first, second)
        return _dot(qs, k_ref[0, :, lo:hi])

    order = list(range(n_tiles))[::-1]
    s_next = scores(order[0])
    for n, t in enumerate(order):
        lo, hi = max(0, t * tq - window), (t + 1) * tq
        s = s_next
        if n + 1 < n_tiles:
            s_next = scores(order[n + 1])
        s = s + bm_ref[0, :, width - (hi - lo):]
        p = jnp.exp2(s - jnp.max(s, axis=-1, keepdims=True)).astype(BF16)
        o_ref[t * tq:hi, :] = _pv_normalised(p, v_ref[0, :, lo:hi], first, tq).astype(BF16)


def _band_bias_mask(rel_table, tq):
    width = A_WINDOW + tq
    chunk_bias = _toeplitz_bias(rel_table, CHUNK, A_BAND, A_BAND - CHUNK) * LOG2E
    rows = [jnp.pad(chunk_bias, ((0, 0), (0, 0), (c * CHUNK, width - A_BAND - c * CHUNK)),
                    constant_values=NEG_INF) for c in range(tq // CHUNK)]
    bm = jnp.concatenate(rows, axis=1)
    return bm.reshape(bm.shape[0] // 2, 2 * tq, width)


def _band_prompt(q, k, v, rel_table, *, batch, tq):
    t, d = q.shape
    s_len = t // batch
    tq = min(tq, s_len)
    bm = _band_bias_mask(rel_table, tq)
    blk = pl.BlockSpec((s_len, HEAD_PAIR), lambda b, hp: (b, hp))
    kv_blk = pl.BlockSpec((1, HEAD_PAIR, s_len), lambda b, hp: (b, hp, 0))
    kern = functools.partial(_band_prompt_kernel, tq=tq, window=A_WINDOW)
    return pl.pallas_call(
        kern, grid=(batch, d // HEAD_PAIR),
        in_specs=[blk, kv_blk, kv_blk, pl.BlockSpec((1,) + bm.shape[1:], lambda b, hp: (hp, 0, 0))],
        out_specs=blk, out_shape=jax.ShapeDtypeStruct((t, d), BF16),
        compiler_params=_params(2))(q, k, v, bm)


def _rows(parts):
    return jnp.concatenate(parts, axis=0)


def _feature_major_cache(cache):
    *lead, n, heads, head_dim = cache.shape
    nl = len(lead)
    perm = tuple(range(nl)) + (nl + 1, nl + 2, nl)
    return jnp.transpose(cache, perm).reshape(*lead, heads * head_dim, n)


def _token_major_heads(x_t, heads):
    *lead, d, n = x_t.shape
    nl = len(lead)
    perm = tuple(range(nl)) + (nl + 2, nl, nl + 1)
    return jnp.transpose(x_t.reshape(*lead, heads, d // heads, n), perm)


def _band_sample_kernel(q_ref, k_ref, v_ref, ck_ref, cv_ref, bias_ref, o_ref, *, heads):
    n_new, d = q_ref.shape
    head_dim = d // heads
    win = ck_ref.shape[-1]
    cols = [slice(hd * head_dim, (hd + 1) * head_dim) for hd in range(heads)]
    s_c = _rows([_dot(q_ref[:, c], ck_ref[0, 0, c, :].astype(BF16)) for c in cols])
    s_n = _rows([_dot_nt(q_ref[:, c], k_ref[:, c]) for c in cols])
    s_c = s_c + bias_ref[:, :win]
    s_n = s_n + bias_ref[:, win:]
    m = jnp.maximum(jnp.max(s_c, axis=-1, keepdims=True), jnp.max(s_n, axis=-1, keepdims=True))
    p_c, p_n = jnp.exp2(s_c - m), jnp.exp2(s_n - m)
    inv_l = 1.0 / (jnp.sum(p_c, axis=-1, keepdims=True) + jnp.sum(p_n, axis=-1, keepdims=True))
    p_c, p_n = p_c.astype(BF16), p_n.astype(BF16)
    for hd, c in enumerate(cols):
        rows = slice(hd * n_new, (hd + 1) * n_new)
        o = _dot_nt(p_c[rows], cv_ref[0, 0, c, :].astype(BF16)) + _dot(p_n[rows], v_ref[:, c])
        o_ref[:, c] = (o * inv_l[rows]).astype(BF16)


def _band_sample(q, k, v, cache_k, cache_v, rel_table, *, layer, batch):
    t, d = q.shape
    n_new = t // batch
    _, _, win, heads, _ = cache_k.shape
    bias = (_toeplitz_bias(rel_table, n_new, win + n_new, win) * LOG2E).reshape(heads * n_new, win + n_new)
    row = pl.BlockSpec((n_new, d), lambda b: (b, 0))
    cache = pl.BlockSpec((1, 1, d, win), lambda b: (layer, b, 0, 0))
    kern = functools.partial(_band_sample_kernel, heads=heads)
    return pl.pallas_call(
        kern, grid=(batch,),
        in_specs=[row, row, row, cache, cache, _resident(bias.shape)],
        out_specs=row, out_shape=jax.ShapeDtypeStruct((t, d), BF16),
        compiler_params=_params(1))(q, k, v, _feature_major_cache(cache_k),
                                    _feature_major_cache(cache_v), bias)


def _lane_cumsum(x):
    n = x.shape[-1]
    lane = lax.broadcasted_iota(jnp.int32, x.shape, x.ndim - 1)
    shift = 1
    while shift < n:
        x = x + jnp.where(lane >= shift, pltpu.roll(x, shift, x.ndim - 1), 0.0)
        shift *= 2
    return x


def _fox_prompt_kernel(q_ref, k_ref, v_ref, lft_ref, o_ref, *, tq):
    s_len = q_ref.shape[0]
    cum = _lane_cumsum(lft_ref[0, 0]) * LOG2E
    first, second = _head_masks((tq, HEAD_PAIR))
    r = lax.broadcasted_iota(jnp.int32, (2 * tq, tq), 0)
    c = lax.broadcasted_iota(jnp.int32, (2 * tq, tq), 1)
    causal = c <= jnp.where(r >= tq, r - tq, r)
    n_tiles = s_len // tq

    def scores(t):
        qs = _stack_heads(q_ref[t * tq:(t + 1) * tq, :], first, second)
        return _dot(qs, k_ref[0, :, 0:(t + 1) * tq])

    order = list(range(n_tiles))[::-1]
    s_next = scores(order[0])
    for n, t in enumerate(order):
        lo, hi = t * tq, (t + 1) * tq
        s = s_next
        if n + 1 < n_tiles:
            s_next = scores(order[n + 1])
        s = jnp.concatenate([s[:tq] - cum[0:1, 0:hi], s[tq:] - cum[1:2, 0:hi]], axis=0)
        parts = [jnp.where(causal, s[:, lo:hi], NEG_INF)]
        m = jnp.max(parts[0], axis=-1, keepdims=True)
        if t > 0:
            parts.insert(0, s[:, 0:lo])
            m = jnp.maximum(m, jnp.max(parts[0], axis=-1, keepdims=True))
        p = jnp.concatenate([jnp.exp2(x - m) for x in parts], axis=1).astype(BF16)
        o_ref[lo:hi, :] = _pv_normalised(p, v_ref[0, :, 0:hi], first, tq).astype(BF16)


def _fox_prompt(q, k, v, logf_t, *, batch, tq):
    t, d = q.shape
    s_len = t // batch
    tq = min(tq, s_len)
    heads = logf_t.shape[1]
    blk = pl.BlockSpec((s_len, HEAD_PAIR), lambda b, hp: (b, hp))
    kv_blk = pl.BlockSpec((1, HEAD_PAIR, s_len), lambda b, hp: (b, hp, 0))
    kern = functools.partial(_fox_prompt_kernel, tq=tq)
    return pl.pallas_call(
        kern, grid=(batch, d // HEAD_PAIR),
        in_specs=[blk, kv_blk, kv_blk, pl.BlockSpec((1, 1, 2, s_len), lambda b, hp: (b, hp, 0, 0))],
        out_specs=blk, out_shape=jax.ShapeDtypeStruct((t, d), BF16),
        compiler_params=_params(2))(q, k, v, logf_t.reshape(batch, heads // 2, 2, s_len))


def _logf_cumsum_kernel(lf_ref, cum_ref):
    cum_ref[0] = _lane_cumsum(lf_ref[0]) * LOG2E


def _logf_cumsum(logf_t):
    blk = pl.BlockSpec((1,) + logf_t.shape[1:], lambda b: (b, 0, 0))
    return pl.pallas_call(_logf_cumsum_kernel, grid=(logf_t.shape[0],), in_specs=[blk], out_specs=blk,
                          out_shape=jax.ShapeDtypeStruct(logf_t.shape, F32),
                          compiler_params=_params(1))(logf_t)


FOX_SAMPLE_KEYS = 1024


def _fox_sample_kernel(q_ref, k_ref, v_ref, ck_ref, cv_ref, cc_ref, cn_ref, o_ref,
                       m_ref, l_ref, acc_ref, *, heads):
    n_new, d = q_ref.shape
    head_dim = d // heads
    kt = pl.program_id(1)
    cols = [slice(hd * head_dim, (hd + 1) * head_dim) for hd in range(heads)]
    rows = [slice(hd * n_new, (hd + 1) * n_new) for hd in range(heads)]

    @pl.when(kt == 0)
    def _():
        m_ref[...] = jnp.full(m_ref.shape, NEG_INF, F32)
        l_ref[...] = jnp.zeros(l_ref.shape, F32)
        acc_ref[...] = jnp.zeros(acc_ref.shape, F32)

    def update(s, head_pv):
        m_old = m_ref[...]
        m_new = jnp.maximum(m_old, jnp.max(s, axis=-1, keepdims=True))
        alpha = jnp.exp2(m_old - m_new)
        p = jnp.exp2(s - m_new)
        l_ref[...] = alpha * l_ref[...] + jnp.sum(p, axis=-1, keepdims=True)
        p = p.astype(BF16)
        acc_ref[...] = alpha * acc_ref[...] + _rows([head_pv(hd, p[r]) for hd, r in enumerate(rows)])
        m_ref[...] = m_new

    update(_rows([_dot(q_ref[:, c], ck_ref[0, c, :].astype(BF16)) - cc_ref[0, hd:hd + 1, :]
                  for hd, c in enumerate(cols)]),
           lambda hd, p: _dot_nt(p, cv_ref[0, cols[hd], :].astype(BF16)))

    @pl.when(kt == pl.num_programs(1) - 1)
    def _():
        causal = (lax.broadcasted_iota(jnp.int32, (n_new, n_new), 1)
                  <= lax.broadcasted_iota(jnp.int32, (n_new, n_new), 0))
        s = _rows([jnp.where(causal, _dot_nt(q_ref[:, c], k_ref[:, c]) - cn_ref[0, hd:hd + 1, 0:n_new],
                             NEG_INF) for hd, c in enumerate(cols)])
        update(s, lambda hd, p: _dot(p, v_ref[:, cols[hd]]))
        o = acc_ref[...] * (1.0 / l_ref[...])
        for hd, c in enumerate(cols):
            o_ref[:, c] = o[rows[hd]].astype(BF16)


def _fox_sample(q, k, v, cache_k, cache_v, cum, *, batch):
    t, d = q.shape
    n_new = t // batch
    _, past, heads, head_dim = cache_k.shape
    tk = min(FOX_SAMPLE_KEYS, past)
    assert past % tk == 0 and past % LANES == 0 and tk % LANES == 0
    row = pl.BlockSpec((n_new, d), lambda b, kt: (b, 0))
    cache = pl.BlockSpec((1, d, tk), lambda b, kt: (b, 0, kt))
    kern = functools.partial(_fox_sample_kernel, heads=heads)
    return pl.pallas_call(
        kern, grid=(batch, past // tk),
        in_specs=[row, row, row, cache, cache,
                  pl.BlockSpec((1, heads, tk), lambda b, kt: (b, 0, kt)),
                  pl.BlockSpec((1, heads, LANES), lambda b, kt: (b, 0, past // LANES))],
        out_specs=row, out_shape=jax.ShapeDtypeStruct((t, d), BF16),
        scratch_shapes=[pltpu.VMEM((heads * n_new, 1), F32), pltpu.VMEM((heads * n_new, 1), F32),
                        pltpu.VMEM((heads * n_new, head_dim), F32)],
        compiler_params=_params(2))(q, k, v, _feature_major_cache(cache_k),
                                    _feature_major_cache(cache_v), cum, cum)


RET_BLOCK = 256


def _ret_proj_kernel(h_ref, w_ref, cos_ref, sin_ref, q_ref, k_ref, v_ref, g_ref, *,
                     heads, key_dim, val_width):
    h = h_ref[...]
    cos, sin = cos_ref[...], sin_ref[...]
    half = key_dim // 2
    qk_width = heads * key_dim
    k_scale = key_dim ** -0.5
    for part, (ref, scale) in enumerate(((q_ref, 1.0), (k_ref, k_scale))):
        for hd in range(heads):
            c0 = part * qk_width + hd * key_dim
            x = _dot(h, w_ref[:, c0:c0 + key_dim])
            x1, x2 = x[:, :half], x[:, half:]
            o0 = hd * key_dim
            ref[:, o0:o0 + half] = ((x1 * cos - x2 * sin) * scale).astype(BF16)
            ref[:, o0 + half:o0 + key_dim] = ((x2 * cos + x1 * sin) * scale).astype(BF16)
    v0 = 2 * qk_width
    step = 4 * MXU_DIM
    for c in range(0, val_width, step):
        v_ref[:, c:c + step] = _dot(h, w_ref[:, v0 + c:v0 + c + step]).astype(BF16)
        g = _dot(h, w_ref[:, v0 + val_width + c:v0 + val_width + c + step])
        g_ref[:, c:c + step] = (g * jax.nn.sigmoid(g)).astype(BF16)


def _ret_proj(h, w, cos, sin, *, heads, tm):
    t, d = h.shape
    tm = min(tm, t, cos.shape[0])
    key_dim = d // heads
    val_width = 2 * d
    n_period = cos.shape[0] // tm
    row = lambda i: (i, 0)
    tab = pl.BlockSpec((tm, key_dim // 2), lambda i: (i % n_period, 0))
    kern = functools.partial(_ret_proj_kernel, heads=heads, key_dim=key_dim, val_width=val_width)
    return pl.pallas_call(
        kern, grid=(t // tm,),
        in_specs=[pl.BlockSpec((tm, d), row), _resident(w.shape), tab, tab],
        out_specs=[pl.BlockSpec((tm, d), row), pl.BlockSpec((tm, d), row),
                   pl.BlockSpec((tm, val_width), row), pl.BlockSpec((tm, val_width), row)],
        out_shape=[jax.ShapeDtypeStruct((t, d), BF16), jax.ShapeDtypeStruct((t, d), BF16),
                   jax.ShapeDtypeStruct((t, val_width), BF16),
                   jax.ShapeDtypeStruct((t, val_width), BF16)],
        compiler_params=_params(1))(h, w.astype(BF16), cos, sin)


def _retention_kernel(*refs, n_blocks, blk, has_init):
    it = iter(refs)
    q_ref, k_ref, v_ref, g_ref, gn_ref = (next(it) for _ in range(5))
    dec_ref, qd_ref, kd_ref, sd_ref = (next(it) for _ in range(4))
    s0_ref = next(it) if has_init else None
    y_ref, so_ref, st_ref = next(it), next(it), next(it)

    if has_init:
        st_ref[...] = s0_ref[0, 0]
    else:
        st_ref[...] = jnp.zeros_like(st_ref)
    decay = dec_ref[0]
    q_decay, k_decay, s_decay = qd_ref[0], kd_ref[0], sd_ref[0]
    for c in range(n_blocks):
        rows = slice(c * blk, (c + 1) * blk)
        q, k, v = q_ref[rows, :], k_ref[rows, :], v_ref[rows, :]
        state = st_ref[...]
        scores = (_dot_nt(q, k) * decay).astype(BF16)
        o = _dot(scores, v) + _dot(q, state.astype(BF16)) * q_decay
        k_dec = (k.astype(F32) * k_decay).astype(BF16)
        st_ref[...] = s_decay * state + _dot_tn(k_dec, v)
        mu = jnp.mean(o, axis=-1, keepdims=True)
        cen = o - mu
        var = jnp.mean(cen * cen, axis=-1, keepdims=True)
        y = g_ref[rows, :].astype(F32) * (cen * lax.rsqrt(var + GN_EPS) * gn_ref[...])
        y_ref[rows, :] = y.astype(BF16)
    so_ref[0, 0] = st_ref[...]


def _retention_decays(heads, blk):
    log_gamma = np.log1p(-np.exp2(-5.0 - np.arange(heads, dtype=np.float64)))
    idx = np.arange(blk, dtype=np.float64)
    diff = idx[:, None] - idx[None, :]
    decay = np.where(diff >= 0, np.exp(np.maximum(diff, 0.0)[None] * log_gamma[:, None, None]), 0.0)
    q_decay = np.exp((idx + 1.0)[None, :] * log_gamma[:, None])[..., None]
    k_decay = np.exp((blk - 1.0 - idx)[None, :] * log_gamma[:, None])[..., None]
    s_decay = np.exp(blk * log_gamma)[:, None, None]
    return tuple(jnp.asarray(a, F32) for a in (decay, q_decay, k_decay, s_decay))


def _retention(q, k, v, gate, gn_g, state0, *, batch, heads):
    t, d = q.shape
    s_len = t // batch
    key_dim = d // heads
    val_dim = v.shape[1] // heads
    blk = min(RET_BLOCK, s_len)
    decay, q_decay, k_decay, s_decay = _retention_decays(heads, blk)
    qk_blk = pl.BlockSpec((s_len, key_dim), lambda b, h: (b, h))
    v_blk = pl.BlockSpec((s_len, val_dim), lambda b, h: (b, h))
    per_head = lambda shape: pl.BlockSpec((1,) + shape, lambda b, h: (h, 0, 0))
    state_blk = pl.BlockSpec((1, 1, key_dim, val_dim), lambda b, h: (b, h, 0, 0))
    in_specs = [qk_blk, qk_blk, v_blk, v_blk, pl.BlockSpec((1, val_dim), lambda b, h: (0, h)),
                per_head((blk, blk)), per_head((blk, 1)), per_head((blk, 1)), per_head((1, 1))]
    args = [q, k, v, gate, gn_g.reshape(1, -1).astype(F32), decay, q_decay, k_decay, s_decay]
    if state0 is not None:
        in_specs.append(state_blk)
        args.append(state0)
    kern = functools.partial(_retention_kernel, n_blocks=s_len // blk, blk=blk,
                             has_init=state0 is not None)
    return pl.pallas_call(
        kern, grid=(batch, heads), in_specs=in_specs, out_specs=[v_blk, state_blk],
        out_shape=[jax.ShapeDtypeStruct(v.shape, BF16),
                   jax.ShapeDtypeStruct((batch, heads, key_dim, val_dim), F32)],
        scratch_shapes=[pltpu.VMEM((key_dim, val_dim), F32)],
        compiler_params=_params(2))(*args)


def _rotary_tables(pos, half):
    inv_freq = ROPE_BASE ** (-jnp.arange(half, dtype=F32) / half)
    ang = pos.astype(F32)[:, None] * inv_freq[None, :]
    return jnp.cos(ang), jnp.sin(ang)


TM_PROMPT = 1024
TM_FFN = 1024


def kernel(x_prompt, x_sample, cache_chunk_k, cache_chunk_v, cache_fox_k, cache_fox_v, cache_fox_logf,
           state_ret, norm_g, w_ffn_in, w_ffn_out, a_w_in, a_g_q, a_g_k, a_rel_table, a_w_out,
           b_w_in, b_bias_f, b_g_q, b_g_k, b_w_out, c_w_in, c_gn_g, c_w_out):
    batch, seq, d = x_prompt.shape
    dec_batch, dec_seq, _ = x_sample.shape
    past = cache_fox_k.shape[2]
    depth = norm_g.shape[0]
    xp = x_prompt.reshape(batch * seq, d)
    xs = x_sample.reshape(dec_batch * dec_seq, d)
    head_dim = d // A_HEADS
    outs = {name: [] for name in ("a_ks", "a_vs", "b_fp", "b_ks", "b_vs", "b_fs", "c_sp", "c_ss")}
    kv_prompt = {"a": None, "b": None}

    for i in range(depth):
        g = norm_g[i].astype(F32)
        kind, j = i % N_MIXERS, i // N_MIXERS
        w1_in, w1_out = w_ffn_in[i, 0].astype(BF16), w_ffn_out[i, 0].astype(BF16)
        w2_in, w2_out = w_ffn_in[i, 1].astype(BF16), w_ffn_out[i, 1].astype(BF16)
        gains1 = jnp.stack([g[0], g[0], g[1]])
        gains2 = jnp.stack([g[2], g[3], g[3]])
        xp, hp = _ffn(xp, gains1, w1_in, w1_out, has_post=False, has_hout=True, tm=TM_FFN)
        xs, hs = _ffn(xs, gains1, w1_in, w1_out, has_post=False, has_hout=True, tm=TM_FFN)

        if kind == 0:
            keep = min(A_WINDOW, seq)
            qp, kbp, vbp, kp, vp = _qkv_proj(hp, a_w_in[j], a_g_q[j], a_g_k[j], heads=A_HEADS, seq=seq,
                                             keep=keep, prev=kv_prompt["a"], tm=min(TM_PROMPT, keep))
            kv_prompt["a"] = (kp, vp)
            qs, kbs, vbs, ks, vs = _qkv_proj(hs, a_w_in[j], a_g_q[j], a_g_k[j], heads=A_HEADS,
                                             tm=TM_PROMPT)
            op = _band_prompt(qp, kbp, vbp, a_rel_table[j], batch=batch, tq=256)
            os_ = _band_sample(qs, kbs, vbs, cache_chunk_k, cache_chunk_v, a_rel_table[j],
                               layer=j, batch=dec_batch)
            outs["a_ks"].append(ks.reshape(dec_batch, dec_seq, A_HEADS, head_dim))
            outs["a_vs"].append(vs.reshape(dec_batch, dec_seq, A_HEADS, head_dim))
            w_o = a_w_out[j].astype(BF16)
        elif kind == 1:
            forget = (b_w_in[j][:, 3 * d:], b_bias_f[j])
            qp, kbp, vbp, kp, vp, fpt = _qkv_proj(hp, b_w_in[j], b_g_q[j], b_g_k[j], heads=B_HEADS, seq=seq,
                                                  prev=kv_prompt["b"], forget=forget, tm=TM_PROMPT)
            kv_prompt["b"] = (kp, vp)
            qs, kbs, vbs, ks, vs, fs = _qkv_proj(hs, b_w_in[j], b_g_q[j], b_g_k[j], heads=B_HEADS,
                                                 forget=forget, tm=TM_PROMPT)
            op = _fox_prompt(qp, kbp, vbp, fpt, batch=batch, tq=256)
            total = past + dec_seq
            padded = -(-total // LANES) * LANES
            lf_all = jnp.concatenate([cache_fox_logf[j].astype(F32),
                                      fs.reshape(dec_batch, dec_seq, B_HEADS)], axis=1)
            lf_all = jnp.pad(jnp.swapaxes(lf_all, 1, 2), ((0, 0), (0, 0), (0, padded - total)))
            os_ = _fox_sample(qs, kbs, vbs, cache_fox_k[j], cache_fox_v[j], _logf_cumsum(lf_all),
                              batch=dec_batch)
            outs["b_fp"].append(jnp.swapaxes(fpt, 1, 2))
            outs["b_ks"].append(ks.reshape(dec_batch, dec_seq, B_HEADS, head_dim))
            outs["b_vs"].append(vs.reshape(dec_batch, dec_seq, B_HEADS, head_dim))
            outs["b_fs"].append(fs.reshape(dec_batch, dec_seq, B_HEADS))
            w_o = b_w_out[j].astype(BF16)
        else:
            half = d // C_HEADS // 2
            cos_p, sin_p = _rotary_tables(jnp.arange(seq), half)
            cos_s, sin_s = _rotary_tables(past + jnp.arange(dec_seq), half)
            cos_s, sin_s = jnp.tile(cos_s, (dec_batch, 1)), jnp.tile(sin_s, (dec_batch, 1))
            qp, kp, vp, gp = _ret_proj(hp, c_w_in[j], cos_p, sin_p, heads=C_HEADS, tm=TM_PROMPT)
            qs, ks, vs, gs = _ret_proj(hs, c_w_in[j], cos_s, sin_s, heads=C_HEADS, tm=TM_PROMPT)
            op, sp = _retention(qp, kp, vp, gp, c_gn_g[j], None, batch=batch, heads=C_HEADS)
            os_, ss = _retention(qs, ks, vs, gs, c_gn_g[j], state_ret[j].astype(F32),
                                 batch=dec_batch, heads=C_HEADS)
            outs["c_sp"].append(sp)
            outs["c_ss"].append(ss)
            w_o = c_w_out[j].astype(BF16)

        xp = _ffn(xp, gains2, w2_in, w2_out, mix=(op, w_o), has_post=True, has_hout=False,
                  tm=TM_FFN)
        xs = _ffn(xs, gains2, w2_in, w2_out, mix=(os_, w_o), has_post=True, has_hout=False,
                  tm=TM_FFN)

    stack = lambda name: jnp.stack(outs[name])
    return (xp.reshape(batch, seq, d), xs.reshape(dec_batch, dec_seq, d),
            _token_major_heads(kv_prompt["a"][0], A_HEADS), _token_major_heads(kv_prompt["a"][1], A_HEADS),
            stack("a_ks"), stack("a_vs"),
            _token_major_heads(kv_prompt["b"][0], B_HEADS), _token_major_heads(kv_prompt["b"][1], B_HEADS),
            stack("b_fp"), stack("b_ks"), stack("b_vs"), stack("b_fs"),
            stack("c_sp"), stack("c_ss"))
```

```python
import functools

import numpy as np
import jax
import jax.numpy as jnp
from jax import lax
from jax.experimental import pallas as pl
from jax.experimental.pallas import tpu as pltpu

CHUNK = 64
A_HEADS = 16
A_LEFT_CHUNKS = 8
A_WINDOW = A_LEFT_CHUNKS * CHUNK
A_BAND = A_WINDOW + CHUNK
A_REL_CLIP = 128
B_HEADS = 16
C_HEADS = 4
ROPE_BASE = 10000.0
NORM_EPS = 1e-6
GN_EPS = 1e-5
NEG_INF = -1e30
N_MIXERS = 3
LOG2E = 1.4426950408889634

LANES = 128
HEAD_PAIR = LANES
MXU_DIM = 256
VMEM_LIMIT = 56 * 1024 * 1024

BF16 = jnp.bfloat16
F32 = jnp.float32


def _params(n_axes, vmem=VMEM_LIMIT):
    return pltpu.CompilerParams(dimension_semantics=("arbitrary",) * n_axes,
                                vmem_limit_bytes=vmem)


def _resident(shape):
    nd = len(shape)
    return pl.BlockSpec(shape, lambda *_: (0,) * nd, pipeline_mode=pl.Buffered(1))


def _dot(a, b):
    return jnp.dot(a, b, preferred_element_type=F32)


def _dot_nt(a, b):
    return lax.dot_general(a, b, (((1,), (1,)), ((), ())), preferred_element_type=F32)


def _dot_tn(a, b):
    return lax.dot_general(a, b, (((0,), (0,)), ((), ())), preferred_element_type=F32)


def _rms(x, g):
    return x * lax.rsqrt(jnp.mean(x * x, axis=-1, keepdims=True) + NORM_EPS) * g


def _ffn_kernel(*refs, has_mix, has_post, has_hout, two_streams, bounds, d_ff, sub_rows):
    it = iter(refs)
    x_ref = next(it)
    o_ref = next(it) if has_mix else None
    if two_streams:
        x2_ref = next(it)
        o2_ref = next(it) if has_mix else None
    wo_ref = next(it) if has_mix else None
    g_ref, win_ref, wout_ref = next(it), next(it), next(it)
    xo_ref = next(it)
    ho_ref = next(it) if has_hout else None
    if two_streams:
        xo2_ref = next(it)
        ho2_ref = next(it) if has_hout else None
    rows_args = dict(has_mix=has_mix, has_post=has_post, has_hout=has_hout, bounds=bounds, d_ff=d_ff)
    if not two_streams:
        _ffn_rows(x_ref, o_ref, wo_ref, g_ref, win_ref, wout_ref, xo_ref, ho_ref,
                  sub_rows=sub_rows, **rows_args)
        return
    last = pl.num_programs(0) - 1

    @pl.when(pl.program_id(0) < last)
    def _():
        _ffn_rows(x_ref, o_ref, wo_ref, g_ref, win_ref, wout_ref, xo_ref, ho_ref,
                  sub_rows=sub_rows, **rows_args)

    @pl.when(pl.program_id(0) == last)
    def _():
        _ffn_rows(x2_ref, o2_ref, wo_ref, g_ref, win_ref, wout_ref, xo2_ref, ho2_ref,
                  sub_rows=min(sub_rows, x2_ref.shape[0]), **rows_args)


def _ffn_rows(x_ref, o_ref, wo_ref, g_ref, win_ref, wout_ref, xo_ref, ho_ref, *,
              has_mix, has_post, has_hout, bounds, d_ff, sub_rows):
    starts = list(range(0, x_ref.shape[0], sub_rows))

    def prologue(r):
        rows = slice(r, r + sub_rows)
        x = x_ref[rows, :]
        if has_mix:
            x = x + _dot(o_ref[rows, :], wo_ref[...])
        return x, _rms(x, g_ref[0:1, :]).astype(BF16)

    nxt = prologue(starts[0])
    for n, r in enumerate(starts):
        rows = slice(r, r + sub_rows)
        x, h = nxt
        if n + 1 < len(starts):
            nxt = prologue(starts[n + 1])
        y = None
        for a, b in bounds:
            gate = _dot(h, win_ref[:, a:b])
            up = _dot(h, win_ref[:, d_ff + a:d_ff + b])
            act = (gate * jax.nn.sigmoid(gate) * up).astype(BF16)
            part = _dot(act, wout_ref[a:b, :])
            y = part if y is None else y + part
        x = x + 0.5 * y
        if has_post:
            x = _rms(x, g_ref[1:2, :])
        xo_ref[rows, :] = x
        if has_hout:
            ho_ref[rows, :] = _rms(x, g_ref[2:3, :]).astype(BF16)


FFN_SUB_ROWS = MXU_DIM


def _ffn_bounds(d_ff):
    step = 4 * MXU_DIM
    return tuple((a, min(a + step, d_ff)) for a in range(0, d_ff, step))


def _ffn(x, gains, w_in, w_out, *, mix=None, second=None, has_post, has_hout, tm):
    t, d = x.shape
    d_ff = w_out.shape[0]
    tm = min(tm, t)
    n_steps = t // tm
    row = (lambda i: (i, 0)) if second is None else (lambda i: (jnp.minimum(i, n_steps - 1), 0))
    whole = lambda i: (0, 0)
    in_specs = [pl.BlockSpec((tm, d), row)]
    args = [x]
    if mix is not None:
        o, w_o = mix
        in_specs.append(pl.BlockSpec((tm, o.shape[1]), row))
        args.append(o)
    if second is not None:
        x2, o2 = second
        in_specs.append(pl.BlockSpec(x2.shape, whole, pipeline_mode=pl.Buffered(1)))
        args.append(x2)
        if mix is not None:
            in_specs.append(pl.BlockSpec(o2.shape, whole, pipeline_mode=pl.Buffered(1)))
            args.append(o2)
    if mix is not None:
        in_specs.append(_resident(w_o.shape))
        args.append(w_o)
    in_specs += [_resident(gains.shape), _resident(w_in.shape), _resident(w_out.shape)]
    args += [gains, w_in, w_out]
    out_shape = [jax.ShapeDtypeStruct((t, d), F32)]
    out_specs = [pl.BlockSpec((tm, d), row)]
    if has_hout:
        out_shape.append(jax.ShapeDtypeStruct((t, d), BF16))
        out_specs.append(pl.BlockSpec((tm, d), row))
    if second is not None:
        out_shape.append(jax.ShapeDtypeStruct(x2.shape, F32))
        out_specs.append(pl.BlockSpec(x2.shape, whole))
        if has_hout:
            out_shape.append(jax.ShapeDtypeStruct(x2.shape, BF16))
            out_specs.append(pl.BlockSpec(x2.shape, whole))
    kern = functools.partial(_ffn_kernel, has_mix=mix is not None, has_post=has_post,
                             has_hout=has_hout, two_streams=second is not None,
                             bounds=_ffn_bounds(d_ff), d_ff=d_ff, sub_rows=min(FFN_SUB_ROWS, tm))
    res = pl.pallas_call(kern, grid=(n_steps + (second is not None),), in_specs=in_specs,
                         out_specs=out_specs, out_shape=out_shape, compiler_params=_params(1))(*args)
    return res if len(res) > 1 else res[0]


def _head_mean_square(x, e_ref, head_dim):
    sq = (x * x).astype(BF16)
    parts = [_dot(sq[:, c:c + MXU_DIM], e_ref[...]) for c in range(0, x.shape[1], MXU_DIM)]
    return jnp.concatenate(parts, axis=1) * (1.0 / head_dim)


def _log_sigmoid(x):
    return jnp.minimum(x, 0.0) - jnp.log1p(jnp.exp(-jnp.abs(x)))


PROJ_SUB_ROWS = MXU_DIM


def _qkv_kernel(*refs, d, head_dim, has_f, feature_major, n_prev, tail_tiles, sub_rows):
    it = iter(refs)
    h_ref, wq_ref, wk_ref, wv_ref, e_ref, gq_ref, gk_ref = (next(it) for _ in range(7))
    if has_f:
        wf_ref, bf_ref = next(it), next(it)
    if n_prev:
        pk_ref, pv_ref = next(it), next(it)
    q_ref, kb_ref, vb_ref, kf_ref, vf_ref = (next(it) for _ in range(5))
    lf_ref = next(it) if has_f else None
    heads = d // head_dim

    kept_f32 = []
    for r0 in range(0, h_ref.shape[0], sub_rows):
        rows = slice(r0, r0 + sub_rows)
        h = h_ref[rows, :]
        if feature_major:
            k = _dot_nt(wk_ref[...], h).reshape(heads, head_dim, sub_rows)
            ms = jnp.mean(k * k, axis=1, keepdims=True)
            k = (k * lax.rsqrt(ms + NORM_EPS)).reshape(d, sub_rows) * gk_ref[...]
            v = _dot_nt(wv_ref[...], h)
            out = (0, slice(None), rows)
        else:
            k = _dot(h, wk_ref[...])
            k = k * lax.rsqrt(_head_mean_square(k, e_ref, head_dim) + NORM_EPS) * gk_ref[...]
            v = _dot(h, wv_ref[...])
            out = (rows, slice(None))
        kb_ref[out] = k.astype(BF16)
        vb_ref[out] = v.astype(BF16)
        f32_out = (n_prev,) + out if feature_major else out
        if tail_tiles is None:
            kf_ref[f32_out] = k
            vf_ref[f32_out] = v
        else:
            kept_f32.append((f32_out, k, v))
        q = _dot(h, wq_ref[...])
        q = q * lax.rsqrt(_head_mean_square(q, e_ref, head_dim) + NORM_EPS) * gq_ref[...]
        q_ref[rows, :] = q.astype(BF16)
        if has_f:
            if feature_major:
                lf_ref[out] = _log_sigmoid(_dot_nt(wf_ref[...], h) + bf_ref[...])
            else:
                lf_ref[out] = _log_sigmoid(_dot(h, wf_ref[...]) + bf_ref[...])

    def write_kept():
        for f32_out, k, v in kept_f32:
            kf_ref[f32_out] = k
            vf_ref[f32_out] = v
        if n_prev:
            kf_ref[0:n_prev] = pk_ref[...]
            vf_ref[0:n_prev] = pv_ref[...]

    if tail_tiles is None:
        write_kept()
    else:
        per_seq, kept = tail_tiles
        pl.when(pl.program_id(0) % per_seq >= per_seq - kept)(write_kept)


def _qkv_proj(h, w, g_q, g_k, *, heads, seq=None, keep=None, prev=None, forget=None, tm):
    t, d = h.shape
    tm = min(tm, t)
    head_dim = d // heads
    scale = head_dim ** -0.5 * LOG2E
    blk = np.kron(np.eye(MXU_DIM // head_dim), np.ones((head_dim, head_dim)))
    e = jnp.asarray(blk, BF16)
    gq = (jnp.tile(g_q.astype(F32), heads) * scale).reshape(1, d)
    gk = jnp.tile(g_k.astype(F32), heads)
    wq, wk, wv = (w[:, c * d:(c + 1) * d].astype(BF16) for c in range(3))
    row = lambda i: (i, 0)
    tail_tiles, n_prev = None, 0
    if seq is None:
        gk = gk.reshape(1, d)
        kv_shape, kv_spec = (t, d), pl.BlockSpec((tm, d), row)
        f32_shape, f32_spec = kv_shape, kv_spec
    else:
        keep = seq if keep is None else keep
        tm = min(tm, keep)
        assert seq % tm == 0 and keep % tm == 0
        per_seq, kept = seq // tm, keep // tm
        gk, wk, wv = gk.reshape(d, 1), wk.T, wv.T
        kv_shape = (t // seq, d, seq)
        kv_spec = pl.BlockSpec((1, d, tm), lambda i: (i // per_seq, 0, i % per_seq))
        if kept != per_seq:
            tail_tiles = (per_seq, kept)
        n_prev = 0 if prev is None else prev[0].shape[0]
        f32_shape = (n_prev + 1, t // seq, d, keep)
        f32_index = lambda i: (0, i // per_seq, 0, jnp.maximum(i % per_seq - (per_seq - kept), 0))
        f32_spec = pl.BlockSpec((n_prev + 1, 1, d, tm), f32_index)
    in_specs = [pl.BlockSpec((tm, d), row)] + [_resident((d, d))] * 3 + [
        _resident(e.shape), _resident(gq.shape), _resident(gk.shape)]
    args = [h, wq, wk, wv, e, gq, gk]
    out_shape = ([jax.ShapeDtypeStruct((t, d), BF16)] + [jax.ShapeDtypeStruct(kv_shape, BF16)] * 2
                 + [jax.ShapeDtypeStruct(f32_shape, F32)] * 2)
    out_specs = [pl.BlockSpec((tm, d), row), kv_spec, kv_spec, f32_spec, f32_spec]
    if forget is not None:
        w_f, bias_f = forget
        if seq is None:
            w_f, bias_f = w_f.astype(BF16), bias_f.reshape(1, heads).astype(F32)
            out_shape.append(jax.ShapeDtypeStruct((t, heads), F32))
            out_specs.append(pl.BlockSpec((tm, heads), row))
        else:
            w_f, bias_f = w_f.T.astype(BF16), bias_f.reshape(heads, 1).astype(F32)
            out_shape.append(jax.ShapeDtypeStruct((t // seq, heads, seq), F32))
            out_specs.append(pl.BlockSpec((1, heads, tm), kv_spec.index_map))
        args += [w_f, bias_f]
        in_specs += [_resident(w_f.shape), _resident(bias_f.shape)]
    if n_prev:
        args += list(prev)
        in_specs += [pl.BlockSpec((n_prev, 1, d, tm), f32_index)] * 2
    kern = functools.partial(_qkv_kernel, d=d, head_dim=head_dim, has_f=forget is not None,
                             feature_major=seq is not None, n_prev=n_prev, tail_tiles=tail_tiles,
                             sub_rows=min(PROJ_SUB_ROWS, tm))
    return pl.pallas_call(kern, grid=(t // tm,), in_specs=in_specs, out_specs=out_specs,
                          out_shape=out_shape, compiler_params=_params(1))(*args)


def _head_masks(shape):
    lane = lax.broadcasted_iota(jnp.int32, shape, len(shape) - 1)
    first = lane < (HEAD_PAIR // 2)
    return first, jnp.logical_not(first)


def _softmax_pv(score_parts, value_parts):
    m = None
    for s in score_parts:
        mx = jnp.max(s, axis=-1, keepdims=True)
        m = mx if m is None else jnp.maximum(m, mx)
    acc, l = None, None
    for s, v in zip(score_parts, value_parts):
        p = jnp.exp2(s - m)
        ls = jnp.sum(p, axis=-1, keepdims=True)
        pv = _dot(p.astype(BF16), v)
        acc = pv if acc is None else acc + pv
        l = ls if l is None else l + ls
    return acc * (1.0 / l)


def _stack_heads(q, first, second):
    zero = jnp.zeros_like(q)
    return jnp.concatenate([jnp.where(first, q, zero), jnp.where(second, q, zero)], axis=0)


def _pv_normalised(p, vt, first, tq):
    ones = jnp.ones(vt.shape, vt.dtype)
    o2 = _dot_nt(p, jnp.concatenate([vt, ones], axis=0))
    o2 = o2[:, :HEAD_PAIR] * (1.0 / o2[:, HEAD_PAIR:])
    return jnp.where(first, o2[:tq], o2[tq:])


def _toeplitz_bias(rel_table, n_rows, n_cols, offset):
    n_diag = n_rows + n_cols - 1
    u = np.arange(n_diag) - (n_rows - 1)
    idx = np.clip(offset - u, -A_REL_CLIP, A_REL_CLIP) + A_REL_CLIP
    diag = jnp.pad(rel_table.astype(F32)[:, idx], ((0, 0), (0, 1)))
    heads = diag.shape[0]
    skew = jnp.tile(diag, (1, n_rows))[:, :n_rows * n_diag].reshape(heads, n_rows, n_diag)
    return skew[:, :, n_rows - 1:n_rows - 1 + n_cols]


def _band_prompt_kernel(q_ref, k_ref, v_ref, bm_ref, o_ref, *, tq, window):
    s_len = q_ref.shape[0]
    first, second = _head_masks((tq, HEAD_PAIR))
    width = window + tq
    n_tiles = s_len // tq

    def scores(t):
        lo, hi = max(0, t * tq - window), (t + 1) * tq
        qs = _stack_heads(q_ref[t * tq:hi, :], first, second)
        return _dot(qs, k_ref[0, :, lo:hi])

    order = list(range(n_tiles))[::-1]
    s_next = scores(order[0])
    for n, t in enumerate(order):
        lo, hi = max(0, t * tq - window), (t + 1) * tq
        s = s_next
        if n + 1 < n_tiles:
            s_next = scores(order[n + 1])
        s = s + bm_ref[0, :, width - (hi - lo):]
        p = jnp.exp2(s - jnp.max(s, axis=-1, keepdims=True)).astype(BF16)
        o_ref[t * tq:hi, :] = _pv_normalised(p, v_ref[0, :, lo:hi], first, tq).astype(BF16)


def _band_bias_mask(rel_table, tq):
    width = A_WINDOW + tq
    chunk_bias = _toeplitz_bias(rel_table, CHUNK, A_BAND, A_BAND - CHUNK) * LOG2E
    rows = [jnp.pad(chunk_bias, ((0, 0), (0, 0), (c * CHUNK, width - A_BAND - c * CHUNK)),
                    constant_values=NEG_INF) for c in range(tq // CHUNK)]
    bm = jnp.concatenate(rows, axis=1)
    return bm.reshape(bm.shape[0] // 2, 2 * tq, width)


def _band_prompt(q, k, v, rel_table, *, batch, tq):
    t, d = q.shape
    s_len = t // batch
    tq = min(tq, s_len)
    bm = _band_bias_mask(rel_table, tq)
    blk = pl.BlockSpec((s_len, HEAD_PAIR), lambda b, hp: (b, hp))
    kv_blk = pl.BlockSpec((1, HEAD_PAIR, s_len), lambda b, hp: (b, hp, 0))
    kern = functools.partial(_band_prompt_kernel, tq=tq, window=A_WINDOW)
    return pl.pallas_call(
        kern, grid=(batch, d // HEAD_PAIR),
        in_specs=[blk, kv_blk, kv_blk, pl.BlockSpec((1,) + bm.shape[1:], lambda b, hp: (hp, 0, 0))],
        out_specs=blk, out_shape=jax.ShapeDtypeStruct((t, d), BF16),
        compiler_params=_params(2))(q, k, v, bm)


def _rows(parts):
    return jnp.concatenate(parts, axis=0)


def _feature_major_cache(cache):
    *lead, n, heads, head_dim = cache.shape
    nl = len(lead)
    perm = tuple(range(nl)) + (nl + 1, nl + 2, nl)
    return jnp.transpose(cache, perm).reshape(*lead, heads * head_dim, n)


def _token_major_heads(x_t, heads):
    *lead, d, n = x_t.shape
    nl = len(lead)
    perm = tuple(range(nl)) + (nl + 2, nl, nl + 1)
    return jnp.transpose(x_t.reshape(*lead, heads, d // heads, n), perm)


def _band_sample_kernel(q_ref, k_ref, v_ref, ck_ref, cv_ref, bias_ref, o_ref, *, heads):
    n_new, d = q_ref.shape
    head_dim = d // heads
    win = ck_ref.shape[-1]
    cols = [slice(hd * head_dim, (hd + 1) * head_dim) for hd in range(heads)]
    s_c = _rows([_dot(q_ref[:, c], ck_ref[0, 0, c, :].astype(BF16)) for c in cols])
    s_n = _rows([_dot_nt(q_ref[:, c], k_ref[:, c]) for c in cols])
    s_c = s_c + bias_ref[:, :win]
    s_n = s_n + bias_ref[:, win:]
    m = jnp.maximum(jnp.max(s_c, axis=-1, keepdims=True), jnp.max(s_n, axis=-1, keepdims=True))
    p_c, p_n = jnp.exp2(s_c - m), jnp.exp2(s_n - m)
    inv_l = 1.0 / (jnp.sum(p_c, axis=-1, keepdims=True) + jnp.sum(p_n, axis=-1, keepdims=True))
    p_c, p_n = p_c.astype(BF16), p_n.astype(BF16)
    for hd, c in enumerate(cols):
        rows = slice(hd * n_new, (hd + 1) * n_new)
        o = _dot_nt(p_c[rows], cv_ref[0, 0, c, :].astype(BF16)) + _dot(p_n[rows], v_ref[:, c])
        o_ref[:, c] = (o * inv_l[rows]).astype(BF16)


def _band_sample(q, k, v, cache_k, cache_v, rel_table, *, layer, batch):
    t, d = q.shape
    n_new = t // batch
    _, _, win, heads, _ = cache_k.shape
    bias = (_toeplitz_bias(rel_table, n_new, win + n_new, win) * LOG2E).reshape(heads * n_new, win + n_new)
    row = pl.BlockSpec((n_new, d), lambda b: (b, 0))
    cache = pl.BlockSpec((1, 1, d, win), lambda b: (layer, b, 0, 0))
    kern = functools.partial(_band_sample_kernel, heads=heads)
    return pl.pallas_call(
        kern, grid=(batch,),
        in_specs=[row, row, row, cache, cache, _resident(bias.shape)],
        out_specs=row, out_shape=jax.ShapeDtypeStruct((t, d), BF16),
        compiler_params=_params(1))(q, k, v, _feature_major_cache(cache_k),
                                    _feature_major_cache(cache_v), bias)


def _lane_cumsum(x):
    n = x.shape[-1]
    lane = lax.broadcasted_iota(jnp.int32, x.shape, x.ndim - 1)
    shift = 1
    while shift < n:
        x = x + jnp.where(lane >= shift, pltpu.roll(x, shift, x.ndim - 1), 0.0)
        shift *= 2
    return x


def _fox_prompt_kernel(q_ref, k_ref, v_ref, lft_ref, o_ref, *, tq):
    s_len = q_ref.shape[0]
    cum = _lane_cumsum(lft_ref[0, 0]) * LOG2E
    first, second = _head_masks((tq, HEAD_PAIR))
    r = lax.broadcasted_iota(jnp.int32, (2 * tq, tq), 0)
    c = lax.broadcasted_iota(jnp.int32, (2 * tq, tq), 1)
    causal = c <= jnp.where(r >= tq, r - tq, r)
    n_tiles = s_len // tq

    def scores(t):
        qs = _stack_heads(q_ref[t * tq:(t + 1) * tq, :], first, second)
        return _dot(qs, k_ref[0, :, 0:(t + 1) * tq])

    order = list(range(n_tiles))[::-1]
    s_next = scores(order[0])
    for n, t in enumerate(order):
        lo, hi = t * tq, (t + 1) * tq
        s = s_next
        if n + 1 < n_tiles:
            s_next = scores(order[n + 1])
        s = jnp.concatenate([s[:tq] - cum[0:1, 0:hi], s[tq:] - cum[1:2, 0:hi]], axis=0)
        parts = [jnp.where(causal, s[:, lo:hi], NEG_INF)]
        m = jnp.max(parts[0], axis=-1, keepdims=True)
        if t > 0:
            parts.insert(0, s[:, 0:lo])
            m = jnp.maximum(m, jnp.max(parts[0], axis=-1, keepdims=True))
        p = jnp.concatenate([jnp.exp2(x - m) for x in parts], axis=1).astype(BF16)
        o_ref[lo:hi, :] = _pv_normalised(p, v_ref[0, :, 0:hi], first, tq).astype(BF16)


def _fox_prompt(q, k, v, logf_t, *, batch, tq):
    t, d = q.shape
    s_len = t // batch
    tq = min(tq, s_len)
    heads = logf_t.shape[1]
    blk = pl.BlockSpec((s_len, HEAD_PAIR), lambda b, hp: (b, hp))
    kv_blk = pl.BlockSpec((1, HEAD_PAIR, s_len), lambda b, hp: (b, hp, 0))
    kern = functools.partial(_fox_prompt_kernel, tq=tq)
    return pl.pallas_call(
        kern, grid=(batch, d // HEAD_PAIR),
        in_specs=[blk, kv_blk, kv_blk, pl.BlockSpec((1, 1, 2, s_len), lambda b, hp: (b, hp, 0, 0))],
        out_specs=blk, out_shape=jax.ShapeDtypeStruct((t, d), BF16),
        compiler_params=_params(2))(q, k, v, logf_t.reshape(batch, heads // 2, 2, s_len))


def _logf_cumsum_kernel(lf_ref, cum_ref):
    cum_ref[0] = _lane_cumsum(lf_ref[0]) * LOG2E


def _logf_cumsum(logf_t):
    blk = pl.BlockSpec((1,) + logf_t.shape[1:], lambda b: (b, 0, 0))
    return pl.pallas_call(_logf_cumsum_kernel, grid=(logf_t.shape[0],), in_specs=[blk], out_specs=blk,
                          out_shape=jax.ShapeDtypeStruct(logf_t.shape, F32),
                          compiler_params=_params(1))(logf_t)


FOX_SAMPLE_KEYS = 1024


def _fox_sample_kernel(q_ref, k_ref, v_ref, ck_ref, cv_ref, cc_ref, cn_ref, o_ref,
                       m_ref, l_ref, acc_ref, *, heads):
    n_new, d = q_ref.shape
    head_dim = d // heads
    kt = pl.program_id(1)
    cols = [slice(hd * head_dim, (hd + 1) * head_dim) for hd in range(heads)]
    rows = [slice(hd * n_new, (hd + 1) * n_new) for hd in range(heads)]

    @pl.when(kt == 0)
    def _():
        m_ref[...] = jnp.full(m_ref.shape, NEG_INF, F32)
        l_ref[...] = jnp.zeros(l_ref.shape, F32)
        acc_ref[...] = jnp.zeros(acc_ref.shape, F32)

    def update(s, head_pv):
        m_old = m_ref[...]
        m_new = jnp.maximum(m_old, jnp.max(s, axis=-1, keepdims=True))
        alpha = jnp.exp2(m_old - m_new)
        p = jnp.exp2(s - m_new)
        l_ref[...] = alpha * l_ref[...] + jnp.sum(p, axis=-1, keepdims=True)
        p = p.astype(BF16)
        acc_ref[...] = alpha * acc_ref[...] + _rows([head_pv(hd, p[r]) for hd, r in enumerate(rows)])
        m_ref[...] = m_new

    update(_rows([_dot(q_ref[:, c], ck_ref[0, c, :].astype(BF16)) - cc_ref[0, hd:hd + 1, :]
                  for hd, c in enumerate(cols)]),
           lambda hd, p: _dot_nt(p, cv_ref[0, cols[hd], :].astype(BF16)))

    @pl.when(kt == pl.num_programs(1) - 1)
    def _():
        causal = (lax.broadcasted_iota(jnp.int32, (n_new, n_new), 1)
                  <= lax.broadcasted_iota(jnp.int32, (n_new, n_new), 0))
        s = _rows([jnp.where(causal, _dot_nt(q_ref[:, c], k_ref[:, c]) - cn_ref[0, hd:hd + 1, 0:n_new],
                             NEG_INF) for hd, c in enumerate(cols)])
        update(s, lambda hd, p: _dot(p, v_ref[:, cols[hd]]))
        o = acc_ref[...] * (1.0 / l_ref[...])
        for hd, c in enumerate(cols):
            o_ref[:, c] = o[rows[hd]].astype(BF16)


def _fox_sample(q, k, v, cache_k, cache_v, cum, *, batch):
    t, d = q.shape
    n_new = t // batch
    _, past, heads, head_dim = cache_k.shape
    tk = min(FOX_SAMPLE_KEYS, past)
    assert past % tk == 0 and past % LANES == 0 and tk % LANES == 0
    row = pl.BlockSpec((n_new, d), lambda b, kt: (b, 0))
    cache = pl.BlockSpec((1, d, tk), lambda b, kt: (b, 0, kt))
    kern = functools.partial(_fox_sample_kernel, heads=heads)
    return pl.pallas_call(
        kern, grid=(batch, past // tk),
        in_specs=[row, row, row, cache, cache,
                  pl.BlockSpec((1, heads, tk), lambda b, kt: (b, 0, kt)),
                  pl.BlockSpec((1, heads, LANES), lambda b, kt: (b, 0, past // LANES))],
        out_specs=row, out_shape=jax.ShapeDtypeStruct((t, d), BF16),
        scratch_shapes=[pltpu.VMEM((heads * n_new, 1), F32), pltpu.VMEM((heads * n_new, 1), F32),
                        pltpu.VMEM((heads * n_new, head_dim), F32)],
        compiler_params=_params(2))(q, k, v, _feature_major_cache(cache_k),
                                    _feature_major_cache(cache_v), cum, cum)


RET_BLOCK = 256


def _ret_proj_kernel(h_ref, w_ref, cos_ref, sin_ref, q_ref, k_ref, v_ref, g_ref, *,
                     heads, key_dim, val_width):
    h = h_ref[...]
    cos, sin = cos_ref[...], sin_ref[...]
    half = key_dim // 2
    qk_width = heads * key_dim
    k_scale = key_dim ** -0.5
    for part, (ref, scale) in enumerate(((q_ref, 1.0), (k_ref, k_scale))):
        for hd in range(heads):
            c0 = part * qk_width + hd * key_dim
            x = _dot(h, w_ref[:, c0:c0 + key_dim])
            x1, x2 = x[:, :half], x[:, half:]
            o0 = hd * key_dim
            ref[:, o0:o0 + half] = ((x1 * cos - x2 * sin) * scale).astype(BF16)
            ref[:, o0 + half:o0 + key_dim] = ((x2 * cos + x1 * sin) * scale).astype(BF16)
    v0 = 2 * qk_width
    step = 4 * MXU_DIM
    for c in range(0, val_width, step):
        v_ref[:, c:c + step] = _dot(h, w_ref[:, v0 + c:v0 + c + step]).astype(BF16)
        g = _dot(h, w_ref[:, v0 + val_width + c:v0 + val_width + c + step])
        g_ref[:, c:c + step] = (g * jax.nn.sigmoid(g)).astype(BF16)


def _ret_proj(h, w, cos, sin, *, heads, tm):
    t, d = h.shape
    tm = min(tm, t, cos.shape[0])
    key_dim = d // heads
    val_width = 2 * d
    n_period = cos.shape[0] // tm
    row = lambda i: (i, 0)
    tab = pl.BlockSpec((tm, key_dim // 2), lambda i: (i % n_period, 0))
    kern = functools.partial(_ret_proj_kernel, heads=heads, key_dim=key_dim, val_width=val_width)
    return pl.pallas_call(
        kern, grid=(t // tm,),
        in_specs=[pl.BlockSpec((tm, d), row), _resident(w.shape), tab, tab],
        out_specs=[pl.BlockSpec((tm, d), row), pl.BlockSpec((tm, d), row),
                   pl.BlockSpec((tm, val_width), row), pl.BlockSpec((tm, val_width), row)],
        out_shape=[jax.ShapeDtypeStruct((t, d), BF16), jax.ShapeDtypeStruct((t, d), BF16),
                   jax.ShapeDtypeStruct((t, val_width), BF16),
                   jax.ShapeDtypeStruct((t, val_width), BF16)],
        compiler_params=_params(1))(h, w.astype(BF16), cos, sin)


def _retention_kernel(*refs, n_blocks, blk, has_init):
    it = iter(refs)
    q_ref, k_ref, v_ref, g_ref, gn_ref = (next(it) for _ in range(5))
    dec_ref, qd_ref, kd_ref, sd_ref = (next(it) for _ in range(4))
    s0_ref = next(it) if has_init else None
    y_ref, so_ref, st_ref = next(it), next(it), next(it)

    if has_init:
        st_ref[...] = s0_ref[0, 0]
    else:
        st_ref[...] = jnp.zeros_like(st_ref)
    decay = dec_ref[0]
    q_decay, k_decay, s_decay = qd_ref[0], kd_ref[0], sd_ref[0]
    for c in range(n_blocks):
        rows = slice(c * blk, (c + 1) * blk)
        q, k, v = q_ref[rows, :], k_ref[rows, :], v_ref[rows, :]
        state = st_ref[...]
        scores = (_dot_nt(q, k) * decay).astype(BF16)
        o = _dot(scores, v) + _dot(q, state.astype(BF16)) * q_decay
        k_dec = (k.astype(F32) * k_decay).astype(BF16)
        st_ref[...] = s_decay * state + _dot_tn(k_dec, v)
        mu = jnp.mean(o, axis=-1, keepdims=True)
        cen = o - mu
        var = jnp.mean(cen * cen, axis=-1, keepdims=True)
        y = g_ref[rows, :].astype(F32) * (cen * lax.rsqrt(var + GN_EPS) * gn_ref[...])
        y_ref[rows, :] = y.astype(BF16)
    so_ref[0, 0] = st_ref[...]


def _retention_decays(heads, blk):
    log_gamma = np.log1p(-np.exp2(-5.0 - np.arange(heads, dtype=np.float64)))
    idx = np.arange(blk, dtype=np.float64)
    diff = idx[:, None] - idx[None, :]
    decay = np.where(diff >= 0, np.exp(np.maximum(diff, 0.0)[None] * log_gamma[:, None, None]), 0.0)
    q_decay = np.exp((idx + 1.0)[None, :] * log_gamma[:, None])[..., None]
    k_decay = np.exp((blk - 1.0 - idx)[None, :] * log_gamma[:, None])[..., None]
    s_decay = np.exp(blk * log_gamma)[:, None, None]
    return tuple(jnp.asarray(a, F32) for a in (decay, q_decay, k_decay, s_decay))


def _retention(q, k, v, gate, gn_g, state0, *, batch, heads):
    t, d = q.shape
    s_len = t // batch
    key_dim = d // heads
    val_dim = v.shape[1] // heads
    blk = min(RET_BLOCK, s_len)
    decay, q_decay, k_decay, s_decay = _retention_decays(heads, blk)
    qk_blk = pl.BlockSpec((s_len, key_dim), lambda b, h: (b, h))
    v_blk = pl.BlockSpec((s_len, val_dim), lambda b, h: (b, h))
    per_head = lambda shape: pl.BlockSpec((1,) + shape, lambda b, h: (h, 0, 0))
    state_blk = pl.BlockSpec((1, 1, key_dim, val_dim), lambda b, h: (b, h, 0, 0))
    in_specs = [qk_blk, qk_blk, v_blk, v_blk, pl.BlockSpec((1, val_dim), lambda b, h: (0, h)),
                per_head((blk, blk)), per_head((blk, 1)), per_head((blk, 1)), per_head((1, 1))]
    args = [q, k, v, gate, gn_g.reshape(1, -1).astype(F32), decay, q_decay, k_decay, s_decay]
    if state0 is not None:
        in_specs.append(state_blk)
        args.append(state0)
    kern = functools.partial(_retention_kernel, n_blocks=s_len // blk, blk=blk,
                             has_init=state0 is not None)
    return pl.pallas_call(
        kern, grid=(batch, heads), in_specs=in_specs, out_specs=[v_blk, state_blk],
        out_shape=[jax.ShapeDtypeStruct(v.shape, BF16),
                   jax.ShapeDtypeStruct((batch, heads, key_dim, val_dim), F32)],
        scratch_shapes=[pltpu.VMEM((key_dim, val_dim), F32)],
        compiler_params=_params(2))(*args)


def _rotary_tables(pos, half):
    inv_freq = ROPE_BASE ** (-jnp.arange(half, dtype=F32) / half)
    ang = pos.astype(F32)[:, None] * inv_freq[None, :]
    return jnp.cos(ang), jnp.sin(ang)


TM_PROMPT = 1024
TM_FFN = 1024


def kernel(x_prompt, x_sample, cache_chunk_k, cache_chunk_v, cache_fox_k, cache_fox_v, cache_fox_logf,
           state_ret, norm_g, w_ffn_in, w_ffn_out, a_w_in, a_g_q, a_g_k, a_rel_table, a_w_out,
           b_w_in, b_bias_f, b_g_q, b_g_k, b_w_out, c_w_in, c_gn_g, c_w_out):
    batch, seq, d = x_prompt.shape
    dec_batch, dec_seq, _ = x_sample.shape
    past = cache_fox_k.shape[2]
    depth = norm_g.shape[0]
    xp = x_prompt.reshape(batch * seq, d)
    xs = x_sample.reshape(dec_batch * dec_seq, d)
    head_dim = d // A_HEADS
    outs = {name: [] for name in ("a_ks", "a_vs", "b_fp", "b_ks", "b_vs", "b_fs", "c_sp", "c_ss")}
    kv_prompt = {"a": None, "b": None}

    for i in range(depth):
        g = norm_g[i].astype(F32)
        kind, j = i % N_MIXERS, i // N_MIXERS
        w1_in, w1_out = w_ffn_in[i, 0].astype(BF16), w_ffn_out[i, 0].astype(BF16)
        w2_in, w2_out = w_ffn_in[i, 1].astype(BF16), w_ffn_out[i, 1].astype(BF16)
        gains1 = jnp.stack([g[0], g[0], g[1]])
        gains2 = jnp.stack([g[2], g[3], g[3]])
        xp, hp, xs, hs = _ffn(xp, gains1, w1_in, w1_out, second=(xs, None), has_post=False,
                              has_hout=True, tm=TM_FFN)

        if kind == 0:
            keep = min(A_WINDOW, seq)
            qp, kbp, vbp, kp, vp = _qkv_proj(hp, a_w_in[j], a_g_q[j], a_g_k[j], heads=A_HEADS, seq=seq,
                                             keep=keep, prev=kv_prompt["a"], tm=min(TM_PROMPT, keep))
            kv_prompt["a"] = (kp, vp)
            qs, kbs, vbs, ks, vs = _qkv_proj(hs, a_w_in[j], a_g_q[j], a_g_k[j], heads=A_HEADS,
                                             tm=TM_PROMPT)
            op = _band_prompt(qp, kbp, vbp, a_rel_table[j], batch=batch, tq=256)
            os_ = _band_sample(qs, kbs, vbs, cache_chunk_k, cache_chunk_v, a_rel_table[j],
                               layer=j, batch=dec_batch)
            outs["a_ks"].append(ks.reshape(dec_batch, dec_seq, A_HEADS, head_dim))
            outs["a_vs"].append(vs.reshape(dec_batch, dec_seq, A_HEADS, head_dim))
            w_o = a_w_out[j].astype(BF16)
        elif kind == 1:
            forget = (b_w_in[j][:, 3 * d:], b_bias_f[j])
            qp, kbp, vbp, kp, vp, fpt = _qkv_proj(hp, b_w_in[j], b_g_q[j], b_g_k[j], heads=B_HEADS, seq=seq,
                                                  prev=kv_prompt["b"], forget=forget, tm=TM_PROMPT)
            kv_prompt["b"] = (kp, vp)
            qs, kbs, vbs, ks, vs, fs = _qkv_proj(hs, b_w_in[j], b_g_q[j], b_g_k[j], heads=B_HEADS,
                                                 forget=forget, tm=TM_PROMPT)
            op = _fox_prompt(qp, kbp, vbp, fpt, batch=batch, tq=256)
            total = past + dec_seq
            padded = -(-total // LANES) * LANES
            lf_all = jnp.concatenate([cache_fox_logf[j].astype(F32),
                                      fs.reshape(dec_batch, dec_seq, B_HEADS)], axis=1)
            lf_all = jnp.pad(jnp.swapaxes(lf_all, 1, 2), ((0, 0), (0, 0), (0, padded - total)))
            os_ = _fox_sample(qs, kbs, vbs, cache_fox_k[j], cache_fox_v[j], _logf_cumsum(lf_all),
                              batch=dec_batch)
            outs["b_fp"].append(jnp.swapaxes(fpt, 1, 2))
            outs["b_ks"].append(ks.reshape(dec_batch, dec_seq, B_HEADS, head_dim))
            outs["b_vs"].append(vs.reshape(dec_batch, dec_seq, B_HEADS, head_dim))
            outs["b_fs"].append(fs.reshape(dec_batch, dec_seq, B_HEADS))
            w_o = b_w_out[j].astype(BF16)
        else:
            half = d // C_HEADS // 2
            cos_p, sin_p = _rotary_tables(jnp.arange(seq), half)
            cos_s, sin_s = _rotary_tables(past + jnp.arange(dec_seq), half)
            cos_s, sin_s = jnp.tile(cos_s, (dec_batch, 1)), jnp.tile(sin_s, (dec_batch, 1))
            qp, kp, vp, gp = _ret_proj(hp, c_w_in[j], cos_p, sin_p, heads=C_HEADS, tm=TM_PROMPT)
            qs, ks, vs, gs = _ret_proj(hs, c_w_in[j], cos_s, sin_s, heads=C_HEADS, tm=TM_PROMPT)
            op, sp = _retention(qp, kp, vp, gp, c_gn_g[j], None, batch=batch, heads=C_HEADS)
            os_, ss = _retention(qs, ks, vs, gs, c_gn_g[j], state_ret[j].astype(F32),
                                 batch=dec_batch, heads=C_HEADS)
            outs["c_sp"].append(sp)
            outs["c_ss"].append(ss)
            w_o = c_w_out[j].astype(BF16)

        if op.shape[1] <= d:
            xp, xs = _ffn(xp, gains2, w2_in, w2_out, mix=(op, w_o), second=(xs, os_), has_post=True,
                          has_hout=False, tm=TM_FFN)
        else:
            xp = _ffn(xp, gains2, w2_in, w2_out, mix=(op, w_o), has_post=True, has_hout=False,
                      tm=TM_FFN)
            xs = _ffn(xs, gains2, w2_in, w2_out, mix=(os_, w_o), has_post=True, has_hout=False,
                      tm=TM_FFN)

    stack = lambda name: jnp.stack(outs[name])
    return (xp.reshape(batch, seq, d), xs.reshape(dec_batch, dec_seq, d),
            _token_major_heads(kv_prompt["a"][0], A_HEADS), _token_major_heads(kv_prompt["a"][1], A_HEADS),
            stack("a_ks"), stack("a_vs"),
            _token_major_heads(kv_prompt["b"][0], B_HEADS), _token_major_heads(kv_prompt["b"][1], B_HEADS),
            stack("b_fp"), stack("b_ks"), stack("b_vs"), stack("b_fs"),
            stack("c_sp"), stack("c_ss"))
```
